```python
import jax, jax.numpy as jnp
from jax import lax
import numpy as np

D_MODEL = 1024
BATCH = 4
SEQ = 4096
DEPTH = 1

CHUNK = 64
FOX_HEAD_DIM = 64
FOX_WIDTH = D_MODEL // 2
FOX_HEADS = FOX_WIDTH // FOX_HEAD_DIM
Q_BLOCK = 128
SGU_WIDTH = D_MODEL // 2
SGU_GROUP_DIM = 64
SGU_GROUPS = SGU_WIDTH // SGU_GROUP_DIM
SGU_WINDOW = 128
N_BRANCHES = 2
D_FF = -(-8 * D_MODEL // (3 * 256)) * 256
EPS = 1e-6
FORGET_BIAS = 2.0

Q_OFF = 0
K_OFF = Q_OFF + FOX_WIDTH
V_OFF = K_OFF + FOX_WIDTH
F_OFF = V_OFF + FOX_WIDTH
U_OFF = F_OFF + FOX_HEADS
G_OFF = U_OFF + 2 * SGU_WIDTH
IN_COLS = G_OFF + N_BRANCHES * D_MODEL

kernel_name = "fox_gmlp_gated_hybrid_block"


def rmsnorm(x, g):
    xf = x.astype(jnp.float32)
    y = xf * lax.rsqrt(jnp.mean(xf * xf, axis=-1, keepdims=True) + EPS)
    return (y * g.astype(jnp.float32)).astype(x.dtype)


def layernorm(x, g, b):
    xf = x.astype(jnp.float32)
    mu = jnp.mean(xf, axis=-1, keepdims=True)
    xc = xf - mu
    y = xc * lax.rsqrt(jnp.mean(xc * xc, axis=-1, keepdims=True) + EPS)
    return (y * g.astype(jnp.float32) + b.astype(jnp.float32)).astype(x.dtype)


def forgetting_attention(q, k, v, log_f):
    s_len = q.shape[2]
    d_cum = jnp.cumsum(log_f, axis=-1)
    scale = FOX_HEAD_DIM ** -0.5
    outs = []
    for i in range(s_len // Q_BLOCK):
        q0, q1 = i * Q_BLOCK, (i + 1) * Q_BLOCK
        qb = q[:, :, q0:q1]
        kb = k[:, :, :q1]
        vb = v[:, :, :q1]
        logits = jnp.einsum('bhqd,bhkd->bhqk', qb, kb).astype(jnp.float32) * scale
        logits = logits + d_cum[:, :, q0:q1, None] - d_cum[:, :, None, :q1]
        q_pos = jnp.arange(q0, q1)[:, None]
        k_pos = jnp.arange(q1)[None, :]
        logits = jnp.where(k_pos <= q_pos, logits, -jnp.inf)
        p = jax.nn.softmax(logits, axis=-1)
        outs.append(jnp.einsum('bhqk,bhkd->bhqd', p.astype(vb.dtype), vb))
    return jnp.concatenate(outs, axis=2)


def spatial_gating(uv, g_norm, b_norm, w_spatial, b_spatial):
    bsz, s_len, _ = uv.shape
    u, v = uv[..., :SGU_WIDTH], uv[..., SGU_WIDTH:]
    v = layernorm(v, g_norm, b_norm)
    v = v.reshape(bsz, s_len // SGU_WINDOW, SGU_WINDOW, SGU_GROUPS, SGU_GROUP_DIM)
    t_idx = jnp.arange(SGU_WINDOW)[:, None]
    s_idx = jnp.arange(SGU_WINDOW)[None, :]
    mask = (s_idx // CHUNK) <= (t_idx // CHUNK)
    ws = jnp.where(mask[None], w_spatial, jnp.zeros((), w_spatial.dtype))
    mixed = jnp.einsum('gts,bnsgc->bntgc', ws, v)
    mixed = mixed + jnp.transpose(b_spatial)[None, None, :, :, None]
    return u * mixed.reshape(bsz, s_len, SGU_WIDTH)


def setup_inputs(seed: int = 0) -> dict:
    key = jax.random.key(seed)
    ks = jax.random.split(key, 20)
    f32 = jnp.float32

    def nrm(k, shape, scale):
        return jax.random.normal(k, shape, f32) * scale

    def gain(k, shape):
        return 1.0 + 0.05 * jax.random.normal(k, shape, f32)

    L = DEPTH
    return {
        "x": jax.random.normal(ks[0], (BATCH, SEQ, D_MODEL), f32),
        "g_pre_mix": gain(ks[1], (L, D_MODEL)),
        "w_in": nrm(ks[2], (L, D_MODEL, IN_COLS), D_MODEL ** -0.5),
        "b_forget": FORGET_BIAS + 0.1 * jax.random.normal(ks[3], (L, FOX_HEADS), f32),
        "g_q": gain(ks[4], (L, FOX_HEAD_DIM)),
        "g_k": gain(ks[5], (L, FOX_HEAD_DIM)),
        "g_sgu": gain(ks[6], (L, SGU_WIDTH)),
        "b_sgu": nrm(ks[7], (L, SGU_WIDTH), 0.02),
        "w_spatial": nrm(ks[8], (L, SGU_GROUPS, SGU_WINDOW, SGU_WINDOW), SGU_WINDOW ** -0.5),
        "b_spatial": 1.0 + 0.05 * jax.random.normal(ks[9], (L, SGU_GROUPS, SGU_WINDOW), f32),
        "w_branch_a": nrm(ks[10], (L, FOX_WIDTH, D_MODEL), FOX_WIDTH ** -0.5),
        "w_branch_b": nrm(ks[11], (L, SGU_WIDTH, D_MODEL), SGU_WIDTH ** -0.5),
        "w_out": nrm(ks[12], (L, D_MODEL, D_MODEL), D_MODEL ** -0.5),
        "g_post_mix": gain(ks[13], (L, D_MODEL)),
        "g_pre_ffn": gain(ks[14], (L, D_MODEL)),
        "w_ffn_in": nrm(ks[15], (L, D_MODEL, 2 * D_FF), D_MODEL ** -0.5),
        "w_ffn_down": nrm(ks[16], (L, D_FF, D_MODEL), D_FF ** -0.5),
        "g_post_ffn": gain(ks[17], (L, D_MODEL)),
    }


def reference(x, g_pre_mix, w_in, b_forget, g_q, g_k, g_sgu, b_sgu, w_spatial, b_spatial,
              w_branch_a, w_branch_b, w_out, g_post_mix, g_pre_ffn, w_ffn_in, w_ffn_down,
              g_post_ffn):
    bsz, s_len, _ = x.shape
    for layer in range(DEPTH):
        h = rmsnorm(x, g_pre_mix[layer])
        proj = h @ w_in[layer]

        def heads(t):
            return t.reshape(bsz, s_len, FOX_HEADS, FOX_HEAD_DIM).transpose(0, 2, 1, 3)

        q = rmsnorm(heads(proj[..., Q_OFF:K_OFF]), g_q[layer])
        k = rmsnorm(heads(proj[..., K_OFF:V_OFF]), g_k[layer])
        v = heads(proj[..., V_OFF:F_OFF])
        f_logit = proj[..., F_OFF:U_OFF].astype(jnp.float32) + b_forget[layer].astype(jnp.float32)
        log_f = jnp.transpose(jax.nn.log_sigmoid(f_logit), (0, 2, 1))
        attn = forgetting_attention(q, k, v, log_f)
        attn = attn.transpose(0, 2, 1, 3).reshape(bsz, s_len, FOX_WIDTH)
        y_a = attn @ w_branch_a[layer]

        uv = jax.nn.gelu(proj[..., U_OFF:G_OFF])
        sgu = spatial_gating(uv, g_sgu[layer], b_sgu[layer], w_spatial[layer], b_spatial[layer])
        y_b = sgu @ w_branch_b[layer]

        gates = jax.nn.sigmoid(proj[..., G_OFF:])
        merged = gates[..., :D_MODEL] * y_a + gates[..., D_MODEL:] * y_b
        x = x + rmsnorm(merged @ w_out[layer], g_post_mix[layer])

        h2 = rmsnorm(x, g_pre_ffn[layer])
        gu = h2 @ w_ffn_in[layer]
        ff = (jax.nn.silu(gu[..., :D_FF]) * gu[..., D_FF:]) @ w_ffn_down[layer]
        x = x + rmsnorm(ff, g_post_ffn[layer])
    return x
```

```python
import functools

import numpy as np
import jax
import jax.numpy as jnp
from jax import lax
from jax.experimental import pallas as pl
from jax.experimental.pallas import tpu as pltpu

F32 = jnp.float32
BF16 = jnp.bfloat16

D_MODEL = 1024
CHUNK = 64
HEAD_DIM = 64
FOX_WIDTH = 512
HEADS = 8
SGU_WIDTH = 512
SGU_GROUPS = 8
SGU_WINDOW = 128
D_FF = 2816
EPS = 1e-6

Q_OFF = 0
K_OFF = 512
V_OFF = 1024
F_OFF = 1536
U_OFF = 1544
G_OFF = 2568

LANES = 128
HEAD_PAD = LANES
N_SPLIT = 3
AUG_A = HEAD_DIM
AUG_B = HEAD_DIM + N_SPLIT

TM_QKV = 512
TQ = 512
TK = TM_QKV
TM_MIX = 256
TM_FFN = 256
FF_CHUNK = 1408
NEG = -1e30

VMEM_LIMIT = 56 * 1024 * 1024


def _rms(x, g):
    ms = jnp.mean(x * x, axis=-1, keepdims=True)
    return x * lax.rsqrt(ms + EPS) * g


def _split3(d):
    d1 = d.astype(BF16)
    r1 = d - d1.astype(F32)
    d2 = r1.astype(BF16)
    d3 = (r1 - d2.astype(F32)).astype(BF16)
    return d1, d2, d3


def _qkv_kernel(x_ref, gpre_ref, wqkv_ref, wf_ref, bf_ref, gq_ref, gk_ref, e_ref, tri_ref,
                pq_ref, pk_ref, oq_ref, ok_ref, qT_ref, k_ref, vT_ref, carry_ref):
    @pl.when(pl.program_id(1) == 0)
    def _():
        carry_ref[...] = jnp.zeros_like(carry_ref)

    x = x_ref[0]
    h = _rms(x, gpre_ref[...]).astype(BF16)
    qkv = jnp.dot(h, wqkv_ref[...], preferred_element_type=F32)

    def head_norm(t, g):
        ms = jnp.dot((t * t).astype(BF16), e_ref[...], preferred_element_type=F32)
        return t * lax.rsqrt(ms + EPS) * g

    qn = head_norm(qkv[:, Q_OFF:K_OFF], gq_ref[...] * (HEAD_DIM ** -0.5))
    kn = head_norm(qkv[:, K_OFF:V_OFF], gk_ref[...])
    v = qkv[:, V_OFF:F_OFF]

    f = jnp.dot(h, wf_ref[...], preferred_element_type=F32) + bf_ref[...]
    logf = jnp.minimum(f, 0.0) - jnp.log1p(jnp.exp(-jnp.abs(f)))
    l1, l2, l3 = _split3(logf)
    tri = tri_ref[...]
    d = (jnp.dot(tri, l1, preferred_element_type=F32)
         + jnp.dot(tri, l2, preferred_element_type=F32)
         + jnp.dot(tri, l3, preferred_element_type=F32)) + carry_ref[...]
    carry_ref[...] = d[TM_QKV - 1:TM_QKV, :]

    d1, d2, d3 = _split3(d)
    lane = lax.broadcasted_iota(jnp.int32, d.shape, 1)
    dsel = jnp.where(lane < HEADS, d1, jnp.where(lane < 2 * HEADS, d2, d3))
    aug_q = jnp.dot(dsel, pq_ref[...], preferred_element_type=F32) + oq_ref[...]
    aug_k = jnp.dot(dsel, pk_ref[...], preferred_element_type=F32) + ok_ref[...]

    lo = lax.broadcasted_iota(jnp.int32, (TM_QKV, LANES), 1) < HEAD_DIM
    for i in range(HEADS // 2):
        sl = slice(i * LANES, (i + 1) * LANES)
        for t, aug, is_q in ((qn[:, sl], aug_q, True), (kn[:, sl], aug_k, False)):
            for par in range(2):
                hd = 2 * i + par
                hs = slice(hd * HEAD_PAD, (hd + 1) * HEAD_PAD)
                src = t if par == 0 else pltpu.roll(t, HEAD_DIM, 1)
                full = jnp.where(lo, src, aug[:, hs])
                if is_q:
                    qT_ref[0, hs, :] = full.T.astype(BF16)
                else:
                    k_ref[0, :, hs] = full.astype(BF16)

    vT = v.T
    for hd in range(HEADS):
        vT_ref[0, hd, 0] = vT[hd * HEAD_DIM:(hd + 1) * HEAD_DIM, :].astype(BF16)


def _qkv_call(x, gpre, wqkv, wf, bf, gq, gk, e, tri, pq, pk, oq, ok):
    b, s, _ = x.shape
    n_s = s // TM_QKV
    const = lambda shape: pl.BlockSpec(shape, lambda bi, si: (0,) * len(shape))
    return pl.pallas_call(
        _qkv_kernel,
        grid=(b, n_s),
        in_specs=[
            pl.BlockSpec((1, TM_QKV, D_MODEL), lambda bi, si: (bi, si, 0)),
            const((1, D_MODEL)),
            const((D_MODEL, 3 * FOX_WIDTH)),
            const((D_MODEL, LANES)),
            const((1, LANES)),
            const((1, FOX_WIDTH)),
            const((1, FOX_WIDTH)),
            const((FOX_WIDTH, FOX_WIDTH)),
            const((TM_QKV, TM_QKV)),
            const((LANES, HEADS * HEAD_PAD)),
            const((LANES, HEADS * HEAD_PAD)),
            const((1, HEADS * HEAD_PAD)),
            const((1, HEADS * HEAD_PAD)),
        ],
        out_specs=[
            pl.BlockSpec((1, HEADS * HEAD_PAD, TM_QKV), lambda bi, si: (bi, 0, si)),
            pl.BlockSpec((1, TM_QKV, HEADS * HEAD_PAD), lambda bi, si: (bi, si, 0)),
            pl.BlockSpec((1, HEADS, 1, HEAD_DIM, TM_QKV), lambda bi, si: (bi, 0, si, 0, 0)),
        ],
        out_shape=[
            jax.ShapeDtypeStruct((b, HEADS * HEAD_PAD, s), BF16),
            jax.ShapeDtypeStruct((b, s, HEADS * HEAD_PAD), BF16),
            jax.ShapeDtypeStruct((b, HEADS, n_s, HEAD_DIM, TM_QKV), BF16),
        ],
        scratch_shapes=[pltpu.VMEM((1, LANES), F32)],
        compiler_params=pltpu.CompilerParams(
            dimension_semantics=("arbitrary", "arbitrary"), vmem_limit_bytes=VMEM_LIMIT),
        name="fox_qkv",
    )(x, gpre, wqkv, wf, bf, gq, gk, e, tri, pq, pk, oq, ok)


def _attn_kernel(qT_ref, k_ref, vT_ref, o_ref):
    qi = pl.program_id(2)
    qT = qT_ref[0]

    def step(j, carry, masked):
        m, l, acc = carry
        kj = k_ref[0, pl.ds(pl.multiple_of(j * TK, TK), TK), :]
        s = jnp.dot(kj, qT, preferred_element_type=F32)
        if masked:
            key = lax.broadcasted_iota(jnp.int32, s.shape, 0)
            qry = lax.broadcasted_iota(jnp.int32, s.shape, 1)
            s = jnp.where(key <= qry, s, NEG)
        m_new = jnp.maximum(m, jnp.max(s, axis=0, keepdims=True))
        a = jnp.exp(m - m_new)
        p = jnp.exp(s - m_new)
        l = a * l + jnp.sum(p, axis=0, keepdims=True)
        acc = a * acc + jnp.dot(vT_ref[0, 0, j], p.astype(BF16), preferred_element_type=F32)
        return m_new, l, acc

    init = (jnp.full((1, TQ), NEG, F32), jnp.zeros((1, TQ), F32), jnp.zeros((HEAD_DIM, TQ), F32))
    carry = lax.fori_loop(0, qi, lambda j, c: step(j, c, False), init)
    _, l, acc = step(qi, carry, True)
    o_ref[0] = (acc / l).astype(BF16)


def _attn_call(qT, k, vT):
    b, _, s = qT.shape
    n_k = s // TK
    return pl.pallas_call(
        _attn_kernel,
        grid=(b, HEADS, s // TQ),
        in_specs=[
            pl.BlockSpec((1, HEAD_PAD, TQ), lambda bi, hi, qi: (bi, hi, qi)),
            pl.BlockSpec((1, s, HEAD_PAD), lambda bi, hi, qi: (bi, 0, hi)),
            pl.BlockSpec((1, 1, n_k, HEAD_DIM, TK), lambda bi, hi, qi: (bi, hi, 0, 0, 0)),
        ],
        out_specs=pl.BlockSpec((1, HEAD_DIM, TQ), lambda bi, hi, qi: (bi, hi, qi)),
        out_shape=jax.ShapeDtypeStruct((b, FOX_WIDTH, s), BF16),
        compiler_params=pltpu.CompilerParams(
            dimension_semantics=("arbitrary", "arbitrary", "arbitrary"),
            vmem_limit_bytes=VMEM_LIMIT),
        name="fox_attn",
    )(qT, k, vT)


def _mix_kernel(x_ref, aT_ref, gpre_ref, wuv_ref, wg_ref, gsgu_ref, bsgu_ref, ws_ref, bsp_ref,
                wa_ref, wb_ref, wout_ref, gpost_ref, o_ref, mixed_ref):
    x = x_ref[0]
    h = _rms(x, gpre_ref[...]).astype(BF16)
    uv = jax.nn.gelu(jnp.dot(h, wuv_ref[...], preferred_element_type=F32))
    u = uv[:, :SGU_WIDTH]
    v = uv[:, SGU_WIDTH:]
    mu = jnp.mean(v, axis=-1, keepdims=True)
    vc = v - mu
    var = jnp.mean(vc * vc, axis=-1, keepdims=True)
    vn = vc * lax.rsqrt(var + EPS) * gsgu_ref[...] + bsgu_ref[...]

    t_idx = lax.broadcasted_iota(jnp.int32, (SGU_WINDOW, SGU_WINDOW), 0)
    s_idx = lax.broadcasted_iota(jnp.int32, (SGU_WINDOW, SGU_WINDOW), 1)
    wmask = (s_idx // CHUNK) <= (t_idx // CHUNK)
    lo = lax.broadcasted_iota(jnp.int32, (SGU_WINDOW, LANES), 1) < (SGU_WIDTH // SGU_GROUPS)
    for gp in range(SGU_GROUPS // 2):
        w_pair = jnp.concatenate(
            [jnp.where(wmask, ws_ref[2 * gp], 0.0), jnp.where(wmask, ws_ref[2 * gp + 1], 0.0)],
            axis=1).astype(BF16)
        for w in range(TM_MIX // SGU_WINDOW):
            vp = vn[w * SGU_WINDOW:(w + 1) * SGU_WINDOW, gp * LANES:(gp + 1) * LANES]
            rhs = jnp.concatenate([jnp.where(lo, vp, 0.0), jnp.where(lo, 0.0, vp)],
                                  axis=0).astype(BF16)
            mixed_ref[w * SGU_WINDOW:(w + 1) * SGU_WINDOW, gp * LANES:(gp + 1) * LANES] = (
                jnp.dot(w_pair, rhs, preferred_element_type=F32)
                + bsp_ref[:, gp * LANES:(gp + 1) * LANES])
    sgu = (u * mixed_ref[...]).astype(BF16)

    y_b = jnp.dot(sgu, wb_ref[...], preferred_element_type=F32)
    y_a = lax.dot_general(aT_ref[0], wa_ref[...], (((0,), (0,)), ((), ())),
                          preferred_element_type=F32)
    gates = jax.nn.sigmoid(jnp.dot(h, wg_ref[...], preferred_element_type=F32))
    merged = (gates[:, :D_MODEL] * y_a + gates[:, D_MODEL:] * y_b).astype(BF16)
    o = jnp.dot(merged, wout_ref[...], preferred_element_type=F32)
    o_ref[0] = x + _rms(o, gpost_ref[...])


def _mix_call(x, aT, gpre, wuv, wg, gsgu, bsgu, ws, bsp, wa, wb, wout, gpost):
    b, s, _ = x.shape
    const = lambda shape: pl.BlockSpec(shape, lambda bi, si: (0,) * len(shape))
    return pl.pallas_call(
        _mix_kernel,
        grid=(b, s // TM_MIX),
        in_specs=[
            pl.BlockSpec((1, TM_MIX, D_MODEL), lambda bi, si: (bi, si, 0)),
            pl.BlockSpec((1, FOX_WIDTH, TM_MIX), lambda bi, si: (bi, 0, si)),
            const((1, D_MODEL)),
            const((D_MODEL, 2 * SGU_WIDTH)),
            const((D_MODEL, 2 * D_MODEL)),
            const((1, SGU_WIDTH)),
            const((1, SGU_WIDTH)),
            const((SGU_GROUPS, SGU_WINDOW, SGU_WINDOW)),
            const((SGU_WINDOW, SGU_WIDTH)),
            const((FOX_WIDTH, D_MODEL)),
            const((SGU_WIDTH, D_MODEL)),
            const((D_MODEL, D_MODEL)),
            const((1, D_MODEL)),
        ],
        out_specs=pl.BlockSpec((1, TM_MIX, D_MODEL), lambda bi, si: (bi, si, 0)),
        out_shape=jax.ShapeDtypeStruct(x.shape, F32),
        scratch_shapes=[pltpu.VMEM((TM_MIX, SGU_WIDTH), F32)],
        compiler_params=pltpu.CompilerParams(
            dimension_semantics=("arbitrary", "arbitrary"), vmem_limit_bytes=VMEM_LIMIT),
        name="fox_mix",
    )(x, aT, gpre, wuv, wg, gsgu, bsgu, ws, bsp, wa, wb, wout, gpost)


def _ffn_kernel(x_ref, gpre_ref, win_ref, wdown_ref, gpost_ref, o_ref):
    x = x_ref[...]
    h = _rms(x, gpre_ref[...]).astype(BF16)
    ff = jnp.zeros((TM_FFN, D_MODEL), F32)
    for c in range(D_FF // FF_CHUNK):
        g = jnp.dot(h, win_ref[:, c * FF_CHUNK:(c + 1) * FF_CHUNK], preferred_element_type=F32)
        u = jnp.dot(h, win_ref[:, D_FF + c * FF_CHUNK:D_FF + (c + 1) * FF_CHUNK],
                    preferred_element_type=F32)
        a = (jax.nn.silu(g) * u).astype(BF16)
        ff = ff + jnp.dot(a, wdown_ref[c * FF_CHUNK:(c + 1) * FF_CHUNK, :],
                          preferred_element_type=F32)
    o_ref[...] = x + _rms(ff, gpost_ref[...])


def _ffn_call(x, gpre, win, wdown, gpost):
    n, _ = x.shape
    const = lambda shape: pl.BlockSpec(shape, lambda i: (0,) * len(shape))
    return pl.pallas_call(
        _ffn_kernel,
        grid=(n // TM_FFN,),
        in_specs=[
            pl.BlockSpec((TM_FFN, D_MODEL), lambda i: (i, 0)),
            const((1, D_MODEL)),
            const((D_MODEL, 2 * D_FF)),
            const((D_FF, D_MODEL)),
            const((1, D_MODEL)),
        ],
        out_specs=pl.BlockSpec((TM_FFN, D_MODEL), lambda i: (i, 0)),
        out_shape=jax.ShapeDtypeStruct(x.shape, F32),
        compiler_params=pltpu.CompilerParams(
            dimension_semantics=("arbitrary",), vmem_limit_bytes=VMEM_LIMIT),
        name="fox_ffn",
    )(x, gpre, win, wdown, gpost)


def _constants():
    e = np.kron(np.eye(HEADS, dtype=np.float32),
                np.full((HEAD_DIM, HEAD_DIM), 1.0 / HEAD_DIM, np.float32))
    tri = np.tril(np.ones((TM_QKV, TM_QKV), np.float32))
    pq = np.zeros((LANES, HEADS * HEAD_PAD), np.float32)
    pk = np.zeros((LANES, HEADS * HEAD_PAD), np.float32)
    oq = np.zeros((1, HEADS * HEAD_PAD), np.float32)
    ok = np.zeros((1, HEADS * HEAD_PAD), np.float32)
    for hd in range(HEADS):
        for j in range(N_SPLIT):
            pq[j * HEADS + hd, hd * HEAD_PAD + AUG_A + j] = 1.0
            pk[j * HEADS + hd, hd * HEAD_PAD + AUG_B + j] = -1.0
            oq[0, hd * HEAD_PAD + AUG_B + j] = 1.0
            ok[0, hd * HEAD_PAD + AUG_A + j] = 1.0
    return (jnp.asarray(e, BF16), jnp.asarray(tri, BF16), jnp.asarray(pq, BF16),
            jnp.asarray(pk, BF16), jnp.asarray(oq), jnp.asarray(ok))


def kernel(x, g_pre_mix, w_in, b_forget, g_q, g_k, g_sgu, b_sgu, w_spatial, b_spatial,
           w_branch_a, w_branch_b, w_out, g_post_mix, g_pre_ffn, w_ffn_in, w_ffn_down,
           g_post_ffn):
    bsz, s_len, _ = x.shape
    e, tri, pq, pk, oq, ok = _constants()
    for layer in range(g_pre_mix.shape[0]):
        w = w_in[layer]
        wqkv = w[:, Q_OFF:F_OFF].astype(BF16)
        wf = jnp.pad(jnp.tile(w[:, F_OFF:U_OFF], (1, N_SPLIT)),
                     ((0, 0), (0, LANES - N_SPLIT * HEADS))).astype(BF16)
        bf = jnp.pad(jnp.tile(b_forget[layer], N_SPLIT), (0, LANES - N_SPLIT * HEADS))[None, :]
        wuv = w[:, U_OFF:G_OFF].astype(BF16)
        wg = w[:, G_OFF:].astype(BF16)
        gq = jnp.tile(g_q[layer], HEADS)[None, :]
        gk = jnp.tile(g_k[layer], HEADS)[None, :]
        bsp = jnp.repeat(jnp.transpose(b_spatial[layer]), SGU_WIDTH // SGU_GROUPS, axis=1)

        qT, k, vT = _qkv_call(x, g_pre_mix[layer][None, :], wqkv, wf, bf, gq, gk, e, tri,
                              pq, pk, oq, ok)
        aT = _attn_call(qT, k, vT)
        x = _mix_call(x, aT, g_pre_mix[layer][None, :], wuv, wg, g_sgu[layer][None, :],
                      b_sgu[layer][None, :], w_spatial[layer], bsp,
                      w_branch_a[layer].astype(BF16), w_branch_b[layer].astype(BF16),
                      w_out[layer].astype(BF16), g_post_mix[layer][None, :])
        x = _ffn_call(x.reshape(bsz * s_len, D_MODEL), g_pre_ffn[layer][None, :],
                      w_ffn_in[layer].astype(BF16), w_ffn_down[layer].astype(BF16),
                      g_post_ffn[layer][None, :]).reshape(bsz, s_len, D_MODEL)
    return x
```

```python
import functools

import numpy as np
import jax
import jax.numpy as jnp
from jax import lax
from jax.experimental import pallas as pl
from jax.experimental.pallas import tpu as pltpu

F32 = jnp.float32
BF16 = jnp.bfloat16

D_MODEL = 1024
CHUNK = 64
HEAD_DIM = 64
FOX_WIDTH = 512
HEADS = 8
SGU_WIDTH = 512
SGU_GROUPS = 8
SGU_WINDOW = 128
D_FF = 2816
EPS = 1e-6

Q_OFF = 0
K_OFF = 512
V_OFF = 1024
F_OFF = 1536
U_OFF = 1544
G_OFF = 2568

LANES = 128
HEAD_PAD = LANES
N_SPLIT = 3
AUG_A = HEAD_DIM
AUG_B = HEAD_DIM + N_SPLIT

TM_QKV = 512
TQ = 512
HK = 256
V_ROWS = HEAD_DIM + 16
LOG2E = 1.4426950408889634
TM_MIX = 256
TM_FFN = 256
FF_CHUNK = 1408
NEG = -1e30

VMEM_LIMIT = 56 * 1024 * 1024


def _rms(x, g):
    ms = jnp.mean(x * x, axis=-1, keepdims=True)
    return x * lax.rsqrt(ms + EPS) * g


def _split3(d):
    d1 = d.astype(BF16)
    r1 = d - d1.astype(F32)
    d2 = r1.astype(BF16)
    d3 = (r1 - d2.astype(F32)).astype(BF16)
    return d1, d2, d3


def _qkv_kernel(x_ref, gpre_ref, wqkv_ref, wf_ref, bf_ref, gq_ref, gk_ref, e_ref, tri_ref,
                pq_ref, pk_ref, oq_ref, ok_ref, qT_ref, k_ref, vT_ref, carry_ref):
    @pl.when(pl.program_id(1) == 0)
    def _():
        carry_ref[...] = jnp.zeros_like(carry_ref)

    x = x_ref[0]
    h = _rms(x, gpre_ref[...]).astype(BF16)
    qkv = jnp.dot(h, wqkv_ref[...], preferred_element_type=F32)

    def head_norm(t, g):
        ms = jnp.dot((t * t).astype(BF16), e_ref[...], preferred_element_type=F32)
        return t * lax.rsqrt(ms + EPS) * g

    qn = head_norm(qkv[:, Q_OFF:K_OFF], gq_ref[...] * (HEAD_DIM ** -0.5 * LOG2E))
    kn = head_norm(qkv[:, K_OFF:V_OFF], gk_ref[...])
    v = qkv[:, V_OFF:F_OFF]

    f = jnp.dot(h, wf_ref[...], preferred_element_type=F32) + bf_ref[...]
    logf = jnp.minimum(f, 0.0) - jnp.log1p(jnp.exp(-jnp.abs(f)))
    l1, l2, l3 = _split3(logf)
    tri = tri_ref[...]
    d = (jnp.dot(tri, l1, preferred_element_type=F32)
         + jnp.dot(tri, l2, preferred_element_type=F32)
         + jnp.dot(tri, l3, preferred_element_type=F32)) + carry_ref[...]
    carry_ref[...] = d[TM_QKV - 1:TM_QKV, :]

    d1, d2, d3 = _split3(d * LOG2E)
    lane = lax.broadcasted_iota(jnp.int32, d.shape, 1)
    dsel = jnp.where(lane < HEADS, d1, jnp.where(lane < 2 * HEADS, d2, d3))
    aug_q = jnp.dot(dsel, pq_ref[...], preferred_element_type=F32) + oq_ref[...]
    aug_k = jnp.dot(dsel, pk_ref[...], preferred_element_type=F32) + ok_ref[...]

    lo = lax.broadcasted_iota(jnp.int32, (TM_QKV, LANES), 1) < HEAD_DIM
    for i in range(HEADS // 2):
        sl = slice(i * LANES, (i + 1) * LANES)
        for t, aug, is_q in ((qn[:, sl], aug_q, True), (kn[:, sl], aug_k, False)):
            for par in range(2):
                hd = 2 * i + par
                hs = slice(hd * HEAD_PAD, (hd + 1) * HEAD_PAD)
                src = t if par == 0 else pltpu.roll(t, HEAD_DIM, 1)
                full = jnp.where(lo, src, aug[:, hs])
                if is_q:
                    qT_ref[0, hs, :] = full.T.astype(BF16)
                else:
                    k_ref[0, :, hs] = full.astype(BF16)

    vT = v.T
    ones_rows = (lax.broadcasted_iota(jnp.int32, (V_ROWS - HEAD_DIM, HK), 0) == 0).astype(BF16)
    for hd in range(HEADS):
        for c in range(TM_QKV // HK):
            vT_ref[0, hd, c, :HEAD_DIM, :] = (
                vT[hd * HEAD_DIM:(hd + 1) * HEAD_DIM, c * HK:(c + 1) * HK].astype(BF16))
            vT_ref[0, hd, c, HEAD_DIM:, :] = ones_rows


def _qkv_call(x, gpre, wqkv, wf, bf, gq, gk, e, tri, pq, pk, oq, ok):
    b, s, _ = x.shape
    n_s = s // TM_QKV
    const = lambda shape: pl.BlockSpec(shape, lambda bi, si: (0,) * len(shape))
    return pl.pallas_call(
        _qkv_kernel,
        grid=(b, n_s),
        in_specs=[
            pl.BlockSpec((1, TM_QKV, D_MODEL), lambda bi, si: (bi, si, 0)),
            const((1, D_MODEL)),
            const((D_MODEL, 3 * FOX_WIDTH)),
            const((D_MODEL, LANES)),
            const((1, LANES)),
            const((1, FOX_WIDTH)),
            const((1, FOX_WIDTH)),
            const((FOX_WIDTH, FOX_WIDTH)),
            const((TM_QKV, TM_QKV)),
            const((LANES, HEADS * HEAD_PAD)),
            const((LANES, HEADS * HEAD_PAD)),
            const((1, HEADS * HEAD_PAD)),
            const((1, HEADS * HEAD_PAD)),
        ],
        out_specs=[
            pl.BlockSpec((1, HEADS * HEAD_PAD, TM_QKV), lambda bi, si: (bi, 0, si)),
            pl.BlockSpec((1, TM_QKV, HEADS * HEAD_PAD), lambda bi, si: (bi, si, 0)),
            pl.BlockSpec((1, HEADS, TM_QKV // HK, V_ROWS, HK), lambda bi, si: (bi, 0, si, 0, 0)),
        ],
        out_shape=[
            jax.ShapeDtypeStruct((b, HEADS * HEAD_PAD, s), BF16),
            jax.ShapeDtypeStruct((b, s, HEADS * HEAD_PAD), BF16),
            jax.ShapeDtypeStruct((b, HEADS, s // HK, V_ROWS, HK), BF16),
        ],
        scratch_shapes=[pltpu.VMEM((1, LANES), F32)],
        compiler_params=pltpu.CompilerParams(
            dimension_semantics=("arbitrary", "arbitrary"), vmem_limit_bytes=VMEM_LIMIT),
        name="fox_qkv",
    )(x, gpre, wqkv, wf, bf, gq, gk, e, tri, pq, pk, oq, ok)


def _attn_kernel(qT_ref, k_ref, vT_ref, o_ref, s_a, s_b):
    qi = pl.program_id(2)
    qT = qT_ref[0]

    def stage1(key_start, s_ref, m, mask):
        kj = k_ref[0, pl.ds(key_start, HK), :]
        s = jnp.dot(kj, qT, preferred_element_type=F32)
        if mask is not None:
            s = jnp.where(mask, s, NEG)
        s_ref[...] = s
        return jnp.maximum(m, jnp.max(s, axis=0, keepdims=True))

    def stage2(s_ref, tile, m_old, m_new, acc):
        p = jnp.exp2(s_ref[...] - m_new).astype(BF16)
        a = jnp.exp2(m_old - m_new)
        return a * acc + jnp.dot(vT_ref[0, 0, tile], p, preferred_element_type=F32)

    row = lax.broadcasted_iota(jnp.int32, (HK, TQ), 0)
    col = lax.broadcasted_iota(jnp.int32, (HK, TQ), 1)
    diag0 = pl.multiple_of(qi * TQ, TQ)
    m0 = jnp.full((1, TQ), NEG, F32)
    acc0 = jnp.zeros((V_ROWS, TQ), F32)
    m1 = stage1(diag0, s_a, m0, row <= col)
    m2 = stage1(diag0 + HK, s_b, m1, row + HK <= col)
    acc1 = stage2(s_a, 2 * qi, m0, m1, acc0)

    def body(jj, carry):
        m_prev, m_cur, acc = carry
        tile_b = jnp.where(jj == 0, 2 * qi + 1, 2 * jj - 1)
        m3 = stage1(pl.multiple_of(jj * (2 * HK), 2 * HK), s_a, m_cur, None)
        acc = stage2(s_b, tile_b, m_prev, m_cur, acc)
        m4 = stage1(pl.multiple_of(jj * (2 * HK) + HK, HK), s_b, m3, None)
        acc = stage2(s_a, 2 * jj, m_cur, m3, acc)
        return m3, m4, acc

    m_prev, m_cur, acc = lax.fori_loop(0, qi, body, (m1, m2, acc1))
    tile_b = jnp.where(qi == 0, 1, 2 * qi - 1)
    acc = stage2(s_b, tile_b, m_prev, m_cur, acc)
    o_ref[0] = (acc[:HEAD_DIM] / acc[HEAD_DIM:HEAD_DIM + 1]).astype(BF16)


def _attn_call(qT, k, vT):
    b, _, s = qT.shape
    n_k = s // HK
    return pl.pallas_call(
        _attn_kernel,
        grid=(b, HEADS, s // TQ),
        in_specs=[
            pl.BlockSpec((1, HEAD_PAD, TQ), lambda bi, hi, qi: (bi, hi, qi)),
            pl.BlockSpec((1, s, HEAD_PAD), lambda bi, hi, qi: (bi, 0, hi)),
            pl.BlockSpec((1, 1, n_k, V_ROWS, HK), lambda bi, hi, qi: (bi, hi, 0, 0, 0)),
        ],
        out_specs=pl.BlockSpec((1, HEAD_DIM, TQ), lambda bi, hi, qi: (bi, hi, qi)),
        out_shape=jax.ShapeDtypeStruct((b, FOX_WIDTH, s), BF16),
        scratch_shapes=[pltpu.VMEM((HK, TQ), F32), pltpu.VMEM((HK, TQ), F32)],
        compiler_params=pltpu.CompilerParams(
            dimension_semantics=("arbitrary", "arbitrary", "arbitrary"),
            vmem_limit_bytes=VMEM_LIMIT),
        name="fox_attn",
    )(qT, k, vT)


def _mix_kernel(x_ref, aT_ref, gpre_ref, wuv_ref, wg_ref, gsgu_ref, bsgu_ref, ws_ref, bsp_ref,
                wa_ref, wb_ref, wout_ref, gpost_ref, o_ref, mixed_ref):
    x = x_ref[0]
    h = _rms(x, gpre_ref[...]).astype(BF16)
    uv = jax.nn.gelu(jnp.dot(h, wuv_ref[...], preferred_element_type=F32))
    u = uv[:, :SGU_WIDTH]
    v = uv[:, SGU_WIDTH:]
    mu = jnp.mean(v, axis=-1, keepdims=True)
    vc = v - mu
    var = jnp.mean(vc * vc, axis=-1, keepdims=True)
    vn = vc * lax.rsqrt(var + EPS) * gsgu_ref[...] + bsgu_ref[...]

    t_idx = lax.broadcasted_iota(jnp.int32, (SGU_WINDOW, SGU_WINDOW), 0)
    s_idx = lax.broadcasted_iota(jnp.int32, (SGU_WINDOW, SGU_WINDOW), 1)
    wmask = (s_idx // CHUNK) <= (t_idx // CHUNK)
    lo = lax.broadcasted_iota(jnp.int32, (SGU_WINDOW, LANES), 1) < (SGU_WIDTH // SGU_GROUPS)
    for gp in range(SGU_GROUPS // 2):
        w_pair = jnp.concatenate(
            [jnp.where(wmask, ws_ref[2 * gp], 0.0), jnp.where(wmask, ws_ref[2 * gp + 1], 0.0)],
            axis=1).astype(BF16)
        for w in range(TM_MIX // SGU_WINDOW):
            vp = vn[w * SGU_WINDOW:(w + 1) * SGU_WINDOW, gp * LANES:(gp + 1) * LANES]
            rhs = jnp.concatenate([jnp.where(lo, vp, 0.0), jnp.where(lo, 0.0, vp)],
                                  axis=0).astype(BF16)
            mixed_ref[w * SGU_WINDOW:(w + 1) * SGU_WINDOW, gp * LANES:(gp + 1) * LANES] = (
                jnp.dot(w_pair, rhs, preferred_element_type=F32)
                + bsp_ref[:, gp * LANES:(gp + 1) * LANES])
    sgu = (u * mixed_ref[...]).astype(BF16)

    y_b = jnp.dot(sgu, wb_ref[...], preferred_element_type=F32)
    y_a = lax.dot_general(aT_ref[0], wa_ref[...], (((0,), (0,)), ((), ())),
                          preferred_element_type=F32)
    gates = jax.nn.sigmoid(jnp.dot(h, wg_ref[...], preferred_element_type=F32))
    merged = (gates[:, :D_MODEL] * y_a + gates[:, D_MODEL:] * y_b).astype(BF16)
    o = jnp.dot(merged, wout_ref[...], preferred_element_type=F32)
    o_ref[0] = x + _rms(o, gpost_ref[...])


def _mix_call(x, aT, gpre, wuv, wg, gsgu, bsgu, ws, bsp, wa, wb, wout, gpost):
    b, s, _ = x.shape
    const = lambda shape: pl.BlockSpec(shape, lambda bi, si: (0,) * len(shape))
    return pl.pallas_call(
        _mix_kernel,
        grid=(b, s // TM_MIX),
        in_specs=[
            pl.BlockSpec((1, TM_MIX, D_MODEL), lambda bi, si: (bi, si, 0)),
            pl.BlockSpec((1, FOX_WIDTH, TM_MIX), lambda bi, si: (bi, 0, si)),
            const((1, D_MODEL)),
            const((D_MODEL, 2 * SGU_WIDTH)),
            const((D_MODEL, 2 * D_MODEL)),
            const((1, SGU_WIDTH)),
            const((1, SGU_WIDTH)),
            const((SGU_GROUPS, SGU_WINDOW, SGU_WINDOW)),
            const((SGU_WINDOW, SGU_WIDTH)),
            const((FOX_WIDTH, D_MODEL)),
            const((SGU_WIDTH, D_MODEL)),
            const((D_MODEL, D_MODEL)),
            const((1, D_MODEL)),
        ],
        out_specs=pl.BlockSpec((1, TM_MIX, D_MODEL), lambda bi, si: (bi, si, 0)),
        out_shape=jax.ShapeDtypeStruct(x.shape, F32),
        scratch_shapes=[pltpu.VMEM((TM_MIX, SGU_WIDTH), F32)],
        compiler_params=pltpu.CompilerParams(
            dimension_semantics=("arbitrary", "arbitrary"), vmem_limit_bytes=VMEM_LIMIT),
        name="fox_mix",
    )(x, aT, gpre, wuv, wg, gsgu, bsgu, ws, bsp, wa, wb, wout, gpost)


def _ffn_kernel(x_ref, gpre_ref, win_ref, wdown_ref, gpost_ref, o_ref):
    x = x_ref[...]
    h = _rms(x, gpre_ref[...]).astype(BF16)
    ff = jnp.zeros((TM_FFN, D_MODEL), F32)
    for c in range(D_FF // FF_CHUNK):
        g = jnp.dot(h, win_ref[:, c * FF_CHUNK:(c + 1) * FF_CHUNK], preferred_element_type=F32)
        u = jnp.dot(h, win_ref[:, D_FF + c * FF_CHUNK:D_FF + (c + 1) * FF_CHUNK],
                    preferred_element_type=F32)
        a = (jax.nn.silu(g) * u).astype(BF16)
        ff = ff + jnp.dot(a, wdown_ref[c * FF_CHUNK:(c + 1) * FF_CHUNK, :],
                          preferred_element_type=F32)
    o_ref[...] = x + _rms(ff, gpost_ref[...])


def _ffn_call(x, gpre, win, wdown, gpost):
    n, _ = x.shape
    const = lambda shape: pl.BlockSpec(shape, lambda i: (0,) * len(shape))
    return pl.pallas_call(
        _ffn_kernel,
        grid=(n // TM_FFN,),
        in_specs=[
            pl.BlockSpec((TM_FFN, D_MODEL), lambda i: (i, 0)),
            const((1, D_MODEL)),
            const((D_MODEL, 2 * D_FF)),
            const((D_FF, D_MODEL)),
            const((1, D_MODEL)),
        ],
        out_specs=pl.BlockSpec((TM_FFN, D_MODEL), lambda i: (i, 0)),
        out_shape=jax.ShapeDtypeStruct(x.shape, F32),
        compiler_params=pltpu.CompilerParams(
            dimension_semantics=("arbitrary",), vmem_limit_bytes=VMEM_LIMIT),
        name="fox_ffn",
    )(x, gpre, win, wdown, gpost)


def _constants():
    e = np.kron(np.eye(HEADS, dtype=np.float32),
                np.full((HEAD_DIM, HEAD_DIM), 1.0 / HEAD_DIM, np.float32))
    tri = np.tril(np.ones((TM_QKV, TM_QKV), np.float32))
    pq = np.zeros((LANES, HEADS * HEAD_PAD), np.float32)
    pk = np.zeros((LANES, HEADS * HEAD_PAD), np.float32)
    oq = np.zeros((1, HEADS * HEAD_PAD), np.float32)
    ok = np.zeros((1, HEADS * HEAD_PAD), np.float32)
    for hd in range(HEADS):
        for j in range(N_SPLIT):
            pq[j * HEADS + hd, hd * HEAD_PAD + AUG_A + j] = 1.0
            pk[j * HEADS + hd, hd * HEAD_PAD + AUG_B + j] = -1.0
            oq[0, hd * HEAD_PAD + AUG_B + j] = 1.0
            ok[0, hd * HEAD_PAD + AUG_A + j] = 1.0
    return (jnp.asarray(e, BF16), jnp.asarray(tri, BF16), jnp.asarray(pq, BF16),
            jnp.asarray(pk, BF16), jnp.asarray(oq), jnp.asarray(ok))


def kernel(x, g_pre_mix, w_in, b_forget, g_q, g_k, g_sgu, b_sgu, w_spatial, b_spatial,
           w_branch_a, w_branch_b, w_out, g_post_mix, g_pre_ffn, w_ffn_in, w_ffn_down,
           g_post_ffn):
    bsz, s_len, _ = x.shape
    e, tri, pq, pk, oq, ok = _constants()
    for layer in range(g_pre_mix.shape[0]):
        w = w_in[layer]
        wqkv = w[:, Q_OFF:F_OFF].astype(BF16)
        wf = jnp.pad(jnp.tile(w[:, F_OFF:U_OFF], (1, N_SPLIT)),
                     ((0, 0), (0, LANES - N_SPLIT * HEADS))).astype(BF16)
        bf = jnp.pad(jnp.tile(b_forget[layer], N_SPLIT), (0, LANES - N_SPLIT * HEADS))[None, :]
        wuv = w[:, U_OFF:G_OFF].astype(BF16)
        wg = w[:, G_OFF:].astype(BF16)
        gq = jnp.tile(g_q[layer], HEADS)[None, :]
        gk = jnp.tile(g_k[layer], HEADS)[None, :]
        bsp = jnp.repeat(jnp.transpose(b_spatial[layer]), SGU_WIDTH // SGU_GROUPS, axis=1)

        qT, k, vT = _qkv_call(x, g_pre_mix[layer][None, :], wqkv, wf, bf, gq, gk, e, tri,
                              pq, pk, oq, ok)
        aT = _attn_call(qT, k, vT)
        x = _mix_call(x, aT, g_pre_mix[layer][None, :], wuv, wg, g_sgu[layer][None, :],
                      b_sgu[layer][None, :], w_spatial[layer], bsp,
                      w_branch_a[layer].astype(BF16), w_branch_b[layer].astype(BF16),
                      w_out[layer].astype(BF16), g_post_mix[layer][None, :])
        x = _ffn_call(x.reshape(bsz * s_len, D_MODEL), g_pre_ffn[layer][None, :],
                      w_ffn_in[layer].astype(BF16), w_ffn_down[layer].astype(BF16),
                      g_post_ffn[layer][None, :]).reshape(bsz, s_len, D_MODEL)
    return x
```

```python
import functools

import numpy as np
import jax
import jax.numpy as jnp
from jax import lax
from jax.experimental import pallas as pl
from jax.experimental.pallas import tpu as pltpu

F32 = jnp.float32
BF16 = jnp.bfloat16

D_MODEL = 1024
CHUNK = 64
HEAD_DIM = 64
FOX_WIDTH = 512
HEADS = 8
SGU_WIDTH = 512
SGU_GROUPS = 8
SGU_WINDOW = 128
D_FF = 2816
EPS = 1e-6

Q_OFF = 0
K_OFF = 512
V_OFF = 1024
F_OFF = 1536
U_OFF = 1544
G_OFF = 2568

LANES = 128
HEAD_PAD = LANES
N_SPLIT = 3
AUG_A = HEAD_DIM
AUG_B = HEAD_DIM + N_SPLIT

TM_QKV = 512
TQ = TM_QKV
TK = TM_QKV
N_Q = 4096 // TQ
V_ROWS = HEAD_DIM + 16
LOG2E = 1.4426950408889634
TM_MIX = 256
TM_FFN = 256
FF_CHUNK = 1408
NEG = -1e30

VMEM_LIMIT = 56 * 1024 * 1024


def _rms(x, g):
    ms = jnp.mean(x * x, axis=-1, keepdims=True)
    return x * lax.rsqrt(ms + EPS) * g


def _split3(d):
    d1 = d.astype(BF16)
    r1 = d - d1.astype(F32)
    d2 = r1.astype(BF16)
    d3 = (r1 - d2.astype(F32)).astype(BF16)
    return d1, d2, d3


def _qkv_kernel(x_ref, gpre_ref, wqkv_ref, wf_ref, bf_ref, gq_ref, gk_ref, e_ref, tri_ref,
                pq_ref, pk_ref, oq_ref, ok_ref, qT_ref, k_ref, vT_ref, carry_ref):
    @pl.when(pl.program_id(1) == 0)
    def _():
        carry_ref[...] = jnp.zeros_like(carry_ref)

    x = x_ref[0]
    h = _rms(x, gpre_ref[...]).astype(BF16)
    qkv = jnp.dot(h, wqkv_ref[...], preferred_element_type=F32)

    def head_norm(t, g):
        ms = jnp.dot((t * t).astype(BF16), e_ref[...], preferred_element_type=F32)
        return t * lax.rsqrt(ms + EPS) * g

    qn = head_norm(qkv[:, Q_OFF:K_OFF], gq_ref[...] * (HEAD_DIM ** -0.5 * LOG2E))
    kn = head_norm(qkv[:, K_OFF:V_OFF], gk_ref[...])
    v = qkv[:, V_OFF:F_OFF]

    f = jnp.dot(h, wf_ref[...], preferred_element_type=F32) + bf_ref[...]
    logf = jnp.minimum(f, 0.0) - jnp.log1p(jnp.exp(-jnp.abs(f)))
    l1, l2, l3 = _split3(logf)
    tri = tri_ref[...]
    d = (jnp.dot(tri, l1, preferred_element_type=F32)
         + jnp.dot(tri, l2, preferred_element_type=F32)
         + jnp.dot(tri, l3, preferred_element_type=F32)) + carry_ref[...]
    carry_ref[...] = d[TM_QKV - 1:TM_QKV, :]

    d1, d2, d3 = _split3(d * LOG2E)
    lane = lax.broadcasted_iota(jnp.int32, d.shape, 1)
    dsel = jnp.where(lane < HEADS, d1, jnp.where(lane < 2 * HEADS, d2, d3))
    aug_q = jnp.dot(dsel, pq_ref[...], preferred_element_type=F32) + oq_ref[...]
    aug_k = jnp.dot(dsel, pk_ref[...], preferred_element_type=F32) + ok_ref[...]

    lo = lax.broadcasted_iota(jnp.int32, (TM_QKV, LANES), 1) < HEAD_DIM
    for i in range(HEADS // 2):
        sl = slice(i * LANES, (i + 1) * LANES)
        for t, aug, is_q in ((qn[:, sl], aug_q, True), (kn[:, sl], aug_k, False)):
            for par in range(2):
                hd = 2 * i + par
                hs = slice(hd * HEAD_PAD, (hd + 1) * HEAD_PAD)
                src = t if par == 0 else pltpu.roll(t, HEAD_DIM, 1)
                full = jnp.where(lo, src, aug[:, hs])
                if is_q:
                    qT_ref[0, hd, 0] = full.T.astype(BF16)
                else:
                    k_ref[0, :, hs] = full.astype(BF16)

    vT = v.T
    ones_rows = (lax.broadcasted_iota(jnp.int32, (V_ROWS - HEAD_DIM, TM_QKV), 0) == 0).astype(BF16)
    for hd in range(HEADS):
        vT_ref[0, hd, 0, :HEAD_DIM, :] = vT[hd * HEAD_DIM:(hd + 1) * HEAD_DIM, :].astype(BF16)
        vT_ref[0, hd, 0, HEAD_DIM:, :] = ones_rows


def _qkv_call(x, gpre, wqkv, wf, bf, gq, gk, e, tri, pq, pk, oq, ok):
    b, s, _ = x.shape
    n_s = s // TM_QKV
    const = lambda shape: pl.BlockSpec(shape, lambda bi, si: (0,) * len(shape))
    return pl.pallas_call(
        _qkv_kernel,
        grid=(b, n_s),
        in_specs=[
            pl.BlockSpec((1, TM_QKV, D_MODEL), lambda bi, si: (bi, si, 0)),
            const((1, D_MODEL)),
            const((D_MODEL, 3 * FOX_WIDTH)),
            const((D_MODEL, LANES)),
            const((1, LANES)),
            const((1, FOX_WIDTH)),
            const((1, FOX_WIDTH)),
            const((FOX_WIDTH, FOX_WIDTH)),
            const((TM_QKV, TM_QKV)),
            const((LANES, HEADS * HEAD_PAD)),
            const((LANES, HEADS * HEAD_PAD)),
            const((1, HEADS * HEAD_PAD)),
            const((1, HEADS * HEAD_PAD)),
        ],
        out_specs=[
            pl.BlockSpec((1, HEADS, 1, HEAD_PAD, TM_QKV), lambda bi, si: (bi, 0, si, 0, 0)),
            pl.BlockSpec((1, TM_QKV, HEADS * HEAD_PAD), lambda bi, si: (bi, si, 0)),
            pl.BlockSpec((1, HEADS, 1, V_ROWS, TM_QKV), lambda bi, si: (bi, 0, si, 0, 0)),
        ],
        out_shape=[
            jax.ShapeDtypeStruct((b, HEADS, n_s, HEAD_PAD, TM_QKV), BF16),
            jax.ShapeDtypeStruct((b, s, HEADS * HEAD_PAD), BF16),
            jax.ShapeDtypeStruct((b, HEADS, n_s, V_ROWS, TM_QKV), BF16),
        ],
        scratch_shapes=[pltpu.VMEM((1, LANES), F32)],
        compiler_params=pltpu.CompilerParams(
            dimension_semantics=("arbitrary", "arbitrary"), vmem_limit_bytes=VMEM_LIMIT),
        name="fox_qkv",
    )(x, gpre, wqkv, wf, bf, gq, gk, e, tri, pq, pk, oq, ok)


def _attn_kernel(qT_ref, k_ref, vT_ref, o_ref, sb0, sb1, pb0, pb1, m_ref, acc_ref):
    row = lax.broadcasted_iota(jnp.int32, (TK, TQ), 0)
    col = lax.broadcasted_iota(jnp.int32, (TK, TQ), 1)
    causal = row <= col

    def s1(t, sb, diag):
        qi, j = t
        kj = k_ref[0, pl.ds(pl.multiple_of(j * TK, TK), TK), :]
        s = jnp.dot(kj, qT_ref[0, 0, qi], preferred_element_type=F32)
        if diag:
            s = jnp.where(causal, s, NEG)
        sb[...] = s
        cmax = jnp.max(s, axis=0, keepdims=True)
        if diag:
            m_before = jnp.full((1, TQ), NEG, F32)
            m_after = cmax
        else:
            m_before = m_ref[qi]
            m_after = jnp.maximum(m_before, cmax)
        m_ref[qi] = m_after
        return m_before, m_after

    def s2(sb, pb, ms):
        m_before, m_after = ms
        pb[...] = jnp.exp2(sb[...] - m_after).astype(BF16)
        return jnp.exp2(m_before - m_after)

    def s3(t, pb, a, diag):
        qi, j = t
        pv = jnp.dot(vT_ref[0, 0, j], pb[...], preferred_element_type=F32)
        if diag:
            acc_ref[qi] = pv
        else:
            acc_ref[qi] = a * acc_ref[qi] + pv

    def pipeline(n_tiles, t_first, nxt, diag):
        t0 = t_first
        ms0 = s1(t0, sb0, diag)
        t1 = nxt(t0)
        ms1 = s1(t1, sb1, diag)
        a0 = s2(sb0, pb0, ms0)

        def body(_, c):
            t_a, t_b, ms_b, a_a = c
            t_c = nxt(t_b)
            ms_c = s1(t_c, sb0, diag)
            a_b = s2(sb1, pb1, ms_b)
            s3(t_a, pb0, a_a, diag)
            t_d = nxt(t_c)
            ms_d = s1(t_d, sb1, diag)
            a_c = s2(sb0, pb0, ms_c)
            s3(t_b, pb1, a_b, diag)
            return t_c, t_d, ms_d, a_c

        t_a, t_b, ms_b, a_a = lax.fori_loop(0, (n_tiles - 2) // 2, body, (t0, t1, ms1, a0))
        a_b = s2(sb1, pb1, ms_b)
        s3(t_a, pb0, a_a, diag)
        s3(t_b, pb1, a_b, diag)

    zero = jnp.int32(0)
    pipeline(N_Q, (zero, zero), lambda t: (t[0] + 1, t[1] + 1), True)

    def next_full(t):
        qi, j = t
        wrap = j + 1 == qi
        return jnp.where(wrap, qi + 1, qi), jnp.where(wrap, 0, j + 1)

    pipeline(N_Q * (N_Q - 1) // 2, (zero + 1, zero), next_full, False)

    for qi in range(N_Q):
        acc = acc_ref[qi]
        o_ref[0, 0, qi] = (acc[:HEAD_DIM] / acc[HEAD_DIM:HEAD_DIM + 1]).astype(BF16)


def _attn_call(qT, k, vT):
    b = qT.shape[0]
    s = k.shape[1]
    return pl.pallas_call(
        _attn_kernel,
        grid=(b, HEADS),
        in_specs=[
            pl.BlockSpec((1, 1, N_Q, HEAD_PAD, TQ), lambda bi, hi: (bi, hi, 0, 0, 0)),
            pl.BlockSpec((1, s, HEAD_PAD), lambda bi, hi: (bi, 0, hi)),
            pl.BlockSpec((1, 1, N_Q, V_ROWS, TK), lambda bi, hi: (bi, hi, 0, 0, 0)),
        ],
        out_specs=pl.BlockSpec((1, 1, N_Q, HEAD_DIM, TQ), lambda bi, hi: (bi, hi, 0, 0, 0)),
        out_shape=jax.ShapeDtypeStruct((b, HEADS, N_Q, HEAD_DIM, TQ), BF16),
        scratch_shapes=[
            pltpu.VMEM((TK, TQ), F32), pltpu.VMEM((TK, TQ), F32),
            pltpu.VMEM((TK, TQ), BF16), pltpu.VMEM((TK, TQ), BF16),
            pltpu.VMEM((N_Q, 1, TQ), F32), pltpu.VMEM((N_Q, V_ROWS, TQ), F32),
        ],
        compiler_params=pltpu.CompilerParams(
            dimension_semantics=("arbitrary", "arbitrary"),
            vmem_limit_bytes=VMEM_LIMIT),
        name="fox_attn",
    )(qT, k, vT)


def _mix_kernel(x_ref, aT_ref, gpre_ref, wuv_ref, wg_ref, gsgu_ref, bsgu_ref, ws_ref, bsp_ref,
                wa_ref, wb_ref, wout_ref, gpost_ref, o_ref, mixed_ref):
    x = x_ref[0]
    h = _rms(x, gpre_ref[...]).astype(BF16)
    uv = jax.nn.gelu(jnp.dot(h, wuv_ref[...], preferred_element_type=F32))
    u = uv[:, :SGU_WIDTH]
    v = uv[:, SGU_WIDTH:]
    mu = jnp.mean(v, axis=-1, keepdims=True)
    vc = v - mu
    var = jnp.mean(vc * vc, axis=-1, keepdims=True)
    vn = vc * lax.rsqrt(var + EPS) * gsgu_ref[...] + bsgu_ref[...]

    t_idx = lax.broadcasted_iota(jnp.int32, (SGU_WINDOW, SGU_WINDOW), 0)
    s_idx = lax.broadcasted_iota(jnp.int32, (SGU_WINDOW, SGU_WINDOW), 1)
    wmask = (s_idx // CHUNK) <= (t_idx // CHUNK)
    lo = lax.broadcasted_iota(jnp.int32, (SGU_WINDOW, LANES), 1) < (SGU_WIDTH // SGU_GROUPS)
    for gp in range(SGU_GROUPS // 2):
        w_pair = jnp.concatenate(
            [jnp.where(wmask, ws_ref[2 * gp], 0.0), jnp.where(wmask, ws_ref[2 * gp + 1], 0.0)],
            axis=1).astype(BF16)
        for w in range(TM_MIX // SGU_WINDOW):
            vp = vn[w * SGU_WINDOW:(w + 1) * SGU_WINDOW, gp * LANES:(gp + 1) * LANES]
            rhs = jnp.concatenate([jnp.where(lo, vp, 0.0), jnp.where(lo, 0.0, vp)],
                                  axis=0).astype(BF16)
            mixed_ref[w * SGU_WINDOW:(w + 1) * SGU_WINDOW, gp * LANES:(gp + 1) * LANES] = (
                jnp.dot(w_pair, rhs, preferred_element_type=F32)
                + bsp_ref[:, gp * LANES:(gp + 1) * LANES])
    sgu = (u * mixed_ref[...]).astype(BF16)

    y_b = jnp.dot(sgu, wb_ref[...], preferred_element_type=F32)
    aT = aT_ref[0, :, 0].reshape(FOX_WIDTH, TM_MIX)
    y_a = lax.dot_general(aT, wa_ref[...], (((0,), (0,)), ((), ())),
                          preferred_element_type=F32)
    gates = jax.nn.sigmoid(jnp.dot(h, wg_ref[...], preferred_element_type=F32))
    merged = (gates[:, :D_MODEL] * y_a + gates[:, D_MODEL:] * y_b).astype(BF16)
    o = jnp.dot(merged, wout_ref[...], preferred_element_type=F32)
    o_ref[0] = x + _rms(o, gpost_ref[...])


def _mix_call(x, aT, gpre, wuv, wg, gsgu, bsgu, ws, bsp, wa, wb, wout, gpost):
    b, s, _ = x.shape
    const = lambda shape: pl.BlockSpec(shape, lambda bi, si: (0,) * len(shape))
    return pl.pallas_call(
        _mix_kernel,
        grid=(b, s // TM_MIX),
        in_specs=[
            pl.BlockSpec((1, TM_MIX, D_MODEL), lambda bi, si: (bi, si, 0)),
            pl.BlockSpec((1, HEADS, 1, HEAD_DIM, TM_MIX),
                         lambda bi, si: (bi, 0, si // (TQ // TM_MIX), 0, si % (TQ // TM_MIX))),
            const((1, D_MODEL)),
            const((D_MODEL, 2 * SGU_WIDTH)),
            const((D_MODEL, 2 * D_MODEL)),
            const((1, SGU_WIDTH)),
            const((1, SGU_WIDTH)),
            const((SGU_GROUPS, SGU_WINDOW, SGU_WINDOW)),
            const((SGU_WINDOW, SGU_WIDTH)),
            const((FOX_WIDTH, D_MODEL)),
            const((SGU_WIDTH, D_MODEL)),
            const((D_MODEL, D_MODEL)),
            const((1, D_MODEL)),
        ],
        out_specs=pl.BlockSpec((1, TM_MIX, D_MODEL), lambda bi, si: (bi, si, 0)),
        out_shape=jax.ShapeDtypeStruct(x.shape, F32),
        scratch_shapes=[pltpu.VMEM((TM_MIX, SGU_WIDTH), F32)],
        compiler_params=pltpu.CompilerParams(
            dimension_semantics=("arbitrary", "arbitrary"), vmem_limit_bytes=VMEM_LIMIT),
        name="fox_mix",
    )(x, aT, gpre, wuv, wg, gsgu, bsgu, ws, bsp, wa, wb, wout, gpost)


def _ffn_kernel(x_ref, gpre_ref, win_ref, wdown_ref, gpost_ref, o_ref):
    x = x_ref[...]
    h = _rms(x, gpre_ref[...]).astype(BF16)
    ff = jnp.zeros((TM_FFN, D_MODEL), F32)
    for c in range(D_FF // FF_CHUNK):
        g = jnp.dot(h, win_ref[:, c * FF_CHUNK:(c + 1) * FF_CHUNK], preferred_element_type=F32)
        u = jnp.dot(h, win_ref[:, D_FF + c * FF_CHUNK:D_FF + (c + 1) * FF_CHUNK],
                    preferred_element_type=F32)
        a = (jax.nn.silu(g) * u).astype(BF16)
        ff = ff + jnp.dot(a, wdown_ref[c * FF_CHUNK:(c + 1) * FF_CHUNK, :],
                          preferred_element_type=F32)
    o_ref[...] = x + _rms(ff, gpost_ref[...])


def _ffn_call(x, gpre, win, wdown, gpost):
    n, _ = x.shape
    const = lambda shape: pl.BlockSpec(shape, lambda i: (0,) * len(shape))
    return pl.pallas_call(
        _ffn_kernel,
        grid=(n // TM_FFN,),
        in_specs=[
            pl.BlockSpec((TM_FFN, D_MODEL), lambda i: (i, 0)),
            const((1, D_MODEL)),
            const((D_MODEL, 2 * D_FF)),
            const((D_FF, D_MODEL)),
            const((1, D_MODEL)),
        ],
        out_specs=pl.BlockSpec((TM_FFN, D_MODEL), lambda i: (i, 0)),
        out_shape=jax.ShapeDtypeStruct(x.shape, F32),
        compiler_params=pltpu.CompilerParams(
            dimension_semantics=("arbitrary",), vmem_limit_bytes=VMEM_LIMIT),
        name="fox_ffn",
    )(x, gpre, win, wdown, gpost)


def _constants():
    e = np.kron(np.eye(HEADS, dtype=np.float32),
                np.full((HEAD_DIM, HEAD_DIM), 1.0 / HEAD_DIM, np.float32))
    tri = np.tril(np.ones((TM_QKV, TM_QKV), np.float32))
    pq = np.zeros((LANES, HEADS * HEAD_PAD), np.float32)
    pk = np.zeros((LANES, HEADS * HEAD_PAD), np.float32)
    oq = np.zeros((1, HEADS * HEAD_PAD), np.float32)
    ok = np.zeros((1, HEADS * HEAD_PAD), np.float32)
    for hd in range(HEADS):
        for j in range(N_SPLIT):
            pq[j * HEADS + hd, hd * HEAD_PAD + AUG_A + j] = 1.0
            pk[j * HEADS + hd, hd * HEAD_PAD + AUG_B + j] = -1.0
            oq[0, hd * HEAD_PAD + AUG_B + j] = 1.0
            ok[0, hd * HEAD_PAD + AUG_A + j] = 1.0
    return (jnp.asarray(e, BF16), jnp.asarray(tri, BF16), jnp.asarray(pq, BF16),
            jnp.asarray(pk, BF16), jnp.asarray(oq), jnp.asarray(ok))


def kernel(x, g_pre_mix, w_in, b_forget, g_q, g_k, g_sgu, b_sgu, w_spatial, b_spatial,
           w_branch_a, w_branch_b, w_out, g_post_mix, g_pre_ffn, w_ffn_in, w_ffn_down,
           g_post_ffn):
    bsz, s_len, _ = x.shape
    e, tri, pq, pk, oq, ok = _constants()
    for layer in range(g_pre_mix.shape[0]):
        w = w_in[layer]
        wqkv = w[:, Q_OFF:F_OFF].astype(BF16)
        wf = jnp.pad(jnp.tile(w[:, F_OFF:U_OFF], (1, N_SPLIT)),
                     ((0, 0), (0, LANES - N_SPLIT * HEADS))).astype(BF16)
        bf = jnp.pad(jnp.tile(b_forget[layer], N_SPLIT), (0, LANES - N_SPLIT * HEADS))[None, :]
        wuv = w[:, U_OFF:G_OFF].astype(BF16)
        wg = w[:, G_OFF:].astype(BF16)
        gq = jnp.tile(g_q[layer], HEADS)[None, :]
        gk = jnp.tile(g_k[layer], HEADS)[None, :]
        bsp = jnp.repeat(jnp.transpose(b_spatial[layer]), SGU_WIDTH // SGU_GROUPS, axis=1)

        qT, k, vT = _qkv_call(x, g_pre_mix[layer][None, :], wqkv, wf, bf, gq, gk, e, tri,
                              pq, pk, oq, ok)
        aT = _attn_call(qT, k, vT)
        x = _mix_call(x, aT, g_pre_mix[layer][None, :], wuv, wg, g_sgu[layer][None, :],
                      b_sgu[layer][None, :], w_spatial[layer], bsp,
                      w_branch_a[layer].astype(BF16), w_branch_b[layer].astype(BF16),
                      w_out[layer].astype(BF16), g_post_mix[layer][None, :])
        x = _ffn_call(x.reshape(bsz * s_len, D_MODEL), g_pre_ffn[layer][None, :],
                      w_ffn_in[layer].astype(BF16), w_ffn_down[layer].astype(BF16),
                      g_post_ffn[layer][None, :]).reshape(bsz, s_len, D_MODEL)
    return x
```

```python
import functools

import numpy as np
import jax
import jax.numpy as jnp
from jax import lax
from jax.experimental import pallas as pl
from jax.experimental.pallas import tpu as pltpu

F32 = jnp.float32
BF16 = jnp.bfloat16

D_MODEL = 1024
CHUNK = 64
HEAD_DIM = 64
FOX_WIDTH = 512
HEADS = 8
SGU_WIDTH = 512
SGU_GROUPS = 8
SGU_WINDOW = 128
D_FF = 2816
EPS = 1e-6

Q_OFF = 0
K_OFF = 512
V_OFF = 1024
F_OFF = 1536
U_OFF = 1544
G_OFF = 2568

LANES = 128
HEAD_PAD = LANES
N_SPLIT = 3
AUG_A = HEAD_DIM
AUG_B = HEAD_DIM + N_SPLIT

TM_QKV = 512
TQ = TM_QKV
TK = TM_QKV
N_Q = 4096 // TQ
LAG2 = 2
LAG3 = 4
RING = 3
V_ROWS = HEAD_DIM + 16
LOG2E = 1.4426950408889634
TM_MIX = 256
TM_FFN = 256
FF_CHUNK = 1408
NEG = -1e30

VMEM_LIMIT = 56 * 1024 * 1024


def _rms(x, g):
    ms = jnp.mean(x * x, axis=-1, keepdims=True)
    return x * lax.rsqrt(ms + EPS) * g


def _split3(d):
    d1 = d.astype(BF16)
    r1 = d - d1.astype(F32)
    d2 = r1.astype(BF16)
    d3 = (r1 - d2.astype(F32)).astype(BF16)
    return d1, d2, d3


def _qkv_kernel(x_ref, gpre_ref, wqkv_ref, wf_ref, bf_ref, gq_ref, gk_ref, e_ref, tri_ref,
                pq_ref, pk_ref, oq_ref, ok_ref, qT_ref, k_ref, vT_ref, carry_ref):
    @pl.when(pl.program_id(1) == 0)
    def _():
        carry_ref[...] = jnp.zeros_like(carry_ref)

    x = x_ref[0]
    h = _rms(x, gpre_ref[...]).astype(BF16)
    qkv = jnp.dot(h, wqkv_ref[...], preferred_element_type=F32)

    def head_norm(t, g):
        ms = jnp.dot((t * t).astype(BF16), e_ref[...], preferred_element_type=F32)
        return t * lax.rsqrt(ms + EPS) * g

    qn = head_norm(qkv[:, Q_OFF:K_OFF], gq_ref[...] * (HEAD_DIM ** -0.5 * LOG2E))
    kn = head_norm(qkv[:, K_OFF:V_OFF], gk_ref[...])
    v = qkv[:, V_OFF:F_OFF]

    f = jnp.dot(h, wf_ref[...], preferred_element_type=F32) + bf_ref[...]
    logf = jnp.minimum(f, 0.0) - jnp.log1p(jnp.exp(-jnp.abs(f)))
    l1, l2, l3 = _split3(logf)
    tri = tri_ref[...]
    d = (jnp.dot(tri, l1, preferred_element_type=F32)
         + jnp.dot(tri, l2, preferred_element_type=F32)
         + jnp.dot(tri, l3, preferred_element_type=F32)) + carry_ref[...]
    carry_ref[...] = d[TM_QKV - 1:TM_QKV, :]

    d1, d2, d3 = _split3(d * LOG2E)
    lane = lax.broadcasted_iota(jnp.int32, d.shape, 1)
    dsel = jnp.where(lane < HEADS, d1, jnp.where(lane < 2 * HEADS, d2, d3))
    aug_q = jnp.dot(dsel, pq_ref[...], preferred_element_type=F32) + oq_ref[...]
    aug_k = jnp.dot(dsel, pk_ref[...], preferred_element_type=F32) + ok_ref[...]

    lo = lax.broadcasted_iota(jnp.int32, (TM_QKV, LANES), 1) < HEAD_DIM
    for i in range(HEADS // 2):
        sl = slice(i * LANES, (i + 1) * LANES)
        for t, aug, is_q in ((qn[:, sl], aug_q, True), (kn[:, sl], aug_k, False)):
            for par in range(2):
                hd = 2 * i + par
                hs = slice(hd * HEAD_PAD, (hd + 1) * HEAD_PAD)
                src = t if par == 0 else pltpu.roll(t, HEAD_DIM, 1)
                full = jnp.where(lo, src, aug[:, hs])
                if is_q:
                    qT_ref[0, hd, 0] = full.T.astype(BF16)
                else:
                    k_ref[0, :, hs] = full.astype(BF16)

    vT = v.T
    ones_rows = (lax.broadcasted_iota(jnp.int32, (V_ROWS - HEAD_DIM, TM_QKV), 0) == 0).astype(BF16)
    for hd in range(HEADS):
        vT_ref[0, hd, 0, :HEAD_DIM, :] = vT[hd * HEAD_DIM:(hd + 1) * HEAD_DIM, :].astype(BF16)
        vT_ref[0, hd, 0, HEAD_DIM:, :] = ones_rows


def _qkv_call(x, gpre, wqkv, wf, bf, gq, gk, e, tri, pq, pk, oq, ok):
    b, s, _ = x.shape
    n_s = s // TM_QKV
    const = lambda shape: pl.BlockSpec(shape, lambda bi, si: (0,) * len(shape))
    return pl.pallas_call(
        _qkv_kernel,
        grid=(b, n_s),
        in_specs=[
            pl.BlockSpec((1, TM_QKV, D_MODEL), lambda bi, si: (bi, si, 0)),
            const((1, D_MODEL)),
            const((D_MODEL, 3 * FOX_WIDTH)),
            const((D_MODEL, LANES)),
            const((1, LANES)),
            const((1, FOX_WIDTH)),
            const((1, FOX_WIDTH)),
            const((FOX_WIDTH, FOX_WIDTH)),
            const((TM_QKV, TM_QKV)),
            const((LANES, HEADS * HEAD_PAD)),
            const((LANES, HEADS * HEAD_PAD)),
            const((1, HEADS * HEAD_PAD)),
            const((1, HEADS * HEAD_PAD)),
        ],
        out_specs=[
            pl.BlockSpec((1, HEADS, 1, HEAD_PAD, TM_QKV), lambda bi, si: (bi, 0, si, 0, 0)),
            pl.BlockSpec((1, TM_QKV, HEADS * HEAD_PAD), lambda bi, si: (bi, si, 0)),
            pl.BlockSpec((1, HEADS, 1, V_ROWS, TM_QKV), lambda bi, si: (bi, 0, si, 0, 0)),
        ],
        out_shape=[
            jax.ShapeDtypeStruct((b, HEADS, n_s, HEAD_PAD, TM_QKV), BF16),
            jax.ShapeDtypeStruct((b, s, HEADS * HEAD_PAD), BF16),
            jax.ShapeDtypeStruct((b, HEADS, n_s, V_ROWS, TM_QKV), BF16),
        ],
        scratch_shapes=[pltpu.VMEM((1, LANES), F32)],
        compiler_params=pltpu.CompilerParams(
            dimension_semantics=("arbitrary", "arbitrary"), vmem_limit_bytes=VMEM_LIMIT),
        name="fox_qkv",
    )(x, gpre, wqkv, wf, bf, gq, gk, e, tri, pq, pk, oq, ok)


def _attn_kernel(qT_ref, k_ref, vT_ref, o_ref, *scratch):
    sb = scratch[:RING]
    pb = scratch[RING:2 * RING]
    m_ref, acc_ref, bias_ref = scratch[2 * RING:]

    @pl.when((pl.program_id(0) == 0) & (pl.program_id(1) == 0))
    def _():
        row = lax.broadcasted_iota(jnp.int32, (TK, TQ), 0)
        col = lax.broadcasted_iota(jnp.int32, (TK, TQ), 1)
        bias_ref[...] = jnp.where(row <= col, 0.0, NEG)

    def s1(t, r, diag):
        qi, j = t
        start = j * TK if isinstance(j, int) else pl.multiple_of(j * TK, TK)
        kj = k_ref[0, pl.ds(start, TK), :]
        s = jnp.dot(kj, qT_ref[0, 0, qi], preferred_element_type=F32)
        if diag:
            s = s + bias_ref[...]
        sb[r][...] = s
        cmax = jnp.max(s, axis=0, keepdims=True)
        if diag:
            m_before = jnp.full((1, TQ), NEG, F32)
            m_after = cmax
        else:
            m_before = m_ref[qi]
            m_after = jnp.maximum(m_before, cmax)
        m_ref[qi] = m_after
        return m_before, m_after

    def s2(r, ms):
        m_before, m_after = ms
        pb[r][...] = jnp.exp2(sb[r][...] - m_after).astype(BF16)
        return jnp.exp2(m_before - m_after)

    def s3(t, r, a, diag):
        qi, j = t
        pv = jnp.dot(vT_ref[0, 0, j], pb[r][...], preferred_element_type=F32)
        if diag:
            acc_ref[qi] = pv
        else:
            acc_ref[qi] = a * acc_ref[qi] + pv

    def pipeline(tiles, diag, successor):
        n_tiles = len(tiles)
        ms = {}
        resc = {}

        def slot(u):
            n3 = u - LAG3
            if 0 <= n3 < n_tiles:
                s3(tiles[n3], n3 % RING, resc.pop(n3), diag)
            n2 = u - LAG2
            if 0 <= n2 < n_tiles:
                resc[n2] = s2(n2 % RING, ms.pop(n2))
            if u < n_tiles:
                ms[u] = s1(tiles[u], u % RING, diag)

        loop_lo = LAG3
        n_iter = (n_tiles - loop_lo) // RING
        loop_hi = loop_lo + n_iter * RING
        for u in range(loop_lo):
            slot(u)

        def body(_, carry):
            hist, ms_q, resc_q = carry
            for i in range(RING):
                hist = hist[1:] + (successor(hist[-1]),)
                u = loop_lo + i
                s3(hist[0], (u - LAG3) % RING, resc_q[0], diag)
                resc_q = resc_q[1:] + (s2((u - LAG2) % RING, ms_q[0]),)
                ms_q = ms_q[1:] + (s1(hist[-1], u % RING, diag),)
            return hist, ms_q, resc_q

        if n_iter > 0:
            first = (tiles[0],) + tuple(tiles[:loop_lo])
            hist0 = tuple((jnp.int32(q), jnp.int32(j)) for q, j in first)
            ms0 = tuple(ms.pop(n) for n in range(loop_lo - LAG2, loop_lo))
            resc0 = tuple(resc.pop(n) for n in range(loop_lo - LAG3, loop_lo - LAG2))
            _, ms1, resc1 = lax.fori_loop(0, n_iter, body, (hist0, ms0, resc0))
            ms.update(zip(range(loop_hi - LAG2, loop_hi), ms1))
            resc.update(zip(range(loop_hi - LAG3, loop_hi - LAG2), resc1))
        for u in range(loop_hi, n_tiles + LAG3):
            slot(u)

    def next_full(t):
        qi, j = t
        wrap = j + 1 == qi
        return jnp.where(wrap, qi + 1, qi), jnp.where(wrap, 0, j + 1)

    pipeline([(q, q) for q in range(N_Q)], True, lambda t: (t[0] + 1, t[1] + 1))
    pipeline([(q, j) for q in range(1, N_Q) for j in range(q)], False, next_full)

    for qi in range(N_Q):
        acc = acc_ref[qi]
        o_ref[0, 0, qi] = (acc[:HEAD_DIM] / acc[HEAD_DIM:HEAD_DIM + 1]).astype(BF16)


def _attn_call(qT, k, vT):
    b = qT.shape[0]
    s = k.shape[1]
    return pl.pallas_call(
        _attn_kernel,
        grid=(b, HEADS),
        in_specs=[
            pl.BlockSpec((1, 1, N_Q, HEAD_PAD, TQ), lambda bi, hi: (bi, hi, 0, 0, 0)),
            pl.BlockSpec((1, s, HEAD_PAD), lambda bi, hi: (bi, 0, hi)),
            pl.BlockSpec((1, 1, N_Q, V_ROWS, TK), lambda bi, hi: (bi, hi, 0, 0, 0)),
        ],
        out_specs=pl.BlockSpec((1, 1, N_Q, HEAD_DIM, TQ), lambda bi, hi: (bi, hi, 0, 0, 0)),
        out_shape=jax.ShapeDtypeStruct((b, HEADS, N_Q, HEAD_DIM, TQ), BF16),
        scratch_shapes=(
            [pltpu.VMEM((TK, TQ), F32)] * RING + [pltpu.VMEM((TK, TQ), BF16)] * RING
            + [pltpu.VMEM((N_Q, 1, TQ), F32), pltpu.VMEM((N_Q, V_ROWS, TQ), F32),
               pltpu.VMEM((TK, TQ), F32)]),
        compiler_params=pltpu.CompilerParams(
            dimension_semantics=("arbitrary", "arbitrary"),
            vmem_limit_bytes=VMEM_LIMIT),
        name="fox_attn",
    )(qT, k, vT)


def _mix_kernel(x_ref, aT_ref, gpre_ref, wuv_ref, wg_ref, gsgu_ref, bsgu_ref, ws_ref, bsp_ref,
                wa_ref, wb_ref, wout_ref, gpost_ref, o_ref, mixed_ref):
    x = x_ref[0]
    h = _rms(x, gpre_ref[...]).astype(BF16)
    uv = jax.nn.gelu(jnp.dot(h, wuv_ref[...], preferred_element_type=F32))
    u = uv[:, :SGU_WIDTH]
    v = uv[:, SGU_WIDTH:]
    mu = jnp.mean(v, axis=-1, keepdims=True)
    vc = v - mu
    var = jnp.mean(vc * vc, axis=-1, keepdims=True)
    vn = vc * lax.rsqrt(var + EPS) * gsgu_ref[...] + bsgu_ref[...]

    t_idx = lax.broadcasted_iota(jnp.int32, (SGU_WINDOW, SGU_WINDOW), 0)
    s_idx = lax.broadcasted_iota(jnp.int32, (SGU_WINDOW, SGU_WINDOW), 1)
    wmask = (s_idx // CHUNK) <= (t_idx // CHUNK)
    lo = lax.broadcasted_iota(jnp.int32, (SGU_WINDOW, LANES), 1) < (SGU_WIDTH // SGU_GROUPS)
    for gp in range(SGU_GROUPS // 2):
        w_pair = jnp.concatenate(
            [jnp.where(wmask, ws_ref[2 * gp], 0.0), jnp.where(wmask, ws_ref[2 * gp + 1], 0.0)],
            axis=1).astype(BF16)
        for w in range(TM_MIX // SGU_WINDOW):
            vp = vn[w * SGU_WINDOW:(w + 1) * SGU_WINDOW, gp * LANES:(gp + 1) * LANES]
            rhs = jnp.concatenate([jnp.where(lo, vp, 0.0), jnp.where(lo, 0.0, vp)],
                                  axis=0).astype(BF16)
            mixed_ref[w * SGU_WINDOW:(w + 1) * SGU_WINDOW, gp * LANES:(gp + 1) * LANES] = (
                jnp.dot(w_pair, rhs, preferred_element_type=F32)
                + bsp_ref[:, gp * LANES:(gp + 1) * LANES])
    sgu = (u * mixed_ref[...]).astype(BF16)

    y_b = jnp.dot(sgu, wb_ref[...], preferred_element_type=F32)
    aT = aT_ref[0, :, 0].reshape(FOX_WIDTH, TM_MIX)
    y_a = lax.dot_general(aT, wa_ref[...], (((0,), (0,)), ((), ())),
                          preferred_element_type=F32)
    gates = jax.nn.sigmoid(jnp.dot(h, wg_ref[...], preferred_element_type=F32))
    merged = (gates[:, :D_MODEL] * y_a + gates[:, D_MODEL:] * y_b).astype(BF16)
    o = jnp.dot(merged, wout_ref[...], preferred_element_type=F32)
    o_ref[0] = x + _rms(o, gpost_ref[...])


def _mix_call(x, aT, gpre, wuv, wg, gsgu, bsgu, ws, bsp, wa, wb, wout, gpost):
    b, s, _ = x.shape
    const = lambda shape: pl.BlockSpec(shape, lambda bi, si: (0,) * len(shape))
    return pl.pallas_call(
        _mix_kernel,
        grid=(b, s // TM_MIX),
        in_specs=[
            pl.BlockSpec((1, TM_MIX, D_MODEL), lambda bi, si: (bi, si, 0)),
            pl.BlockSpec((1, HEADS, 1, HEAD_DIM, TM_MIX),
                         lambda bi, si: (bi, 0, si // (TQ // TM_MIX), 0, si % (TQ // TM_MIX))),
            const((1, D_MODEL)),
            const((D_MODEL, 2 * SGU_WIDTH)),
            const((D_MODEL, 2 * D_MODEL)),
            const((1, SGU_WIDTH)),
            const((1, SGU_WIDTH)),
            const((SGU_GROUPS, SGU_WINDOW, SGU_WINDOW)),
            const((SGU_WINDOW, SGU_WIDTH)),
            const((FOX_WIDTH, D_MODEL)),
            const((SGU_WIDTH, D_MODEL)),
            const((D_MODEL, D_MODEL)),
            const((1, D_MODEL)),
        ],
        out_specs=pl.BlockSpec((1, TM_MIX, D_MODEL), lambda bi, si: (bi, si, 0)),
        out_shape=jax.ShapeDtypeStruct(x.shape, F32),
        scratch_shapes=[pltpu.VMEM((TM_MIX, SGU_WIDTH), F32)],
        compiler_params=pltpu.CompilerParams(
            dimension_semantics=("arbitrary", "arbitrary"), vmem_limit_bytes=VMEM_LIMIT),
        name="fox_mix",
    )(x, aT, gpre, wuv, wg, gsgu, bsgu, ws, bsp, wa, wb, wout, gpost)


def _ffn_kernel(x_ref, gpre_ref, win_ref, wdown_ref, gpost_ref, o_ref):
    x = x_ref[...]
    h = _rms(x, gpre_ref[...]).astype(BF16)
    ff = jnp.zeros((TM_FFN, D_MODEL), F32)
    for c in range(D_FF // FF_CHUNK):
        g = jnp.dot(h, win_ref[:, c * FF_CHUNK:(c + 1) * FF_CHUNK], preferred_element_type=F32)
        u = jnp.dot(h, win_ref[:, D_FF + c * FF_CHUNK:D_FF + (c + 1) * FF_CHUNK],
                    preferred_element_type=F32)
        a = (jax.nn.silu(g) * u).astype(BF16)
        ff = ff + jnp.dot(a, wdown_ref[c * FF_CHUNK:(c + 1) * FF_CHUNK, :],
                          preferred_element_type=F32)
    o_ref[...] = x + _rms(ff, gpost_ref[...])


def _ffn_call(x, gpre, win, wdown, gpost):
    n, _ = x.shape
    const = lambda shape: pl.BlockSpec(shape, lambda i: (0,) * len(shape))
    return pl.pallas_call(
        _ffn_kernel,
        grid=(n // TM_FFN,),
        in_specs=[
            pl.BlockSpec((TM_FFN, D_MODEL), lambda i: (i, 0)),
            const((1, D_MODEL)),
            const((D_MODEL, 2 * D_FF)),
            const((D_FF, D_MODEL)),
            const((1, D_MODEL)),
        ],
        out_specs=pl.BlockSpec((TM_FFN, D_MODEL), lambda i: (i, 0)),
        out_shape=jax.ShapeDtypeStruct(x.shape, F32),
        compiler_params=pltpu.CompilerParams(
            dimension_semantics=("arbitrary",), vmem_limit_bytes=VMEM_LIMIT),
        name="fox_ffn",
    )(x, gpre, win, wdown, gpost)


def _constants():
    e = np.kron(np.eye(HEADS, dtype=np.float32),
                np.full((HEAD_DIM, HEAD_DIM), 1.0 / HEAD_DIM, np.float32))
    tri = np.tril(np.ones((TM_QKV, TM_QKV), np.float32))
    pq = np.zeros((LANES, HEADS * HEAD_PAD), np.float32)
    pk = np.zeros((LANES, HEADS * HEAD_PAD), np.float32)
    oq = np.zeros((1, HEADS * HEAD_PAD), np.float32)
    ok = np.zeros((1, HEADS * HEAD_PAD), np.float32)
    for hd in range(HEADS):
        for j in range(N_SPLIT):
            pq[j * HEADS + hd, hd * HEAD_PAD + AUG_A + j] = 1.0
            pk[j * HEADS + hd, hd * HEAD_PAD + AUG_B + j] = -1.0
            oq[0, hd * HEAD_PAD + AUG_B + j] = 1.0
            ok[0, hd * HEAD_PAD + AUG_A + j] = 1.0
    return (jnp.asarray(e, BF16), jnp.asarray(tri, BF16), jnp.asarray(pq, BF16),
            jnp.asarray(pk, BF16), jnp.asarray(oq), jnp.asarray(ok))


def kernel(x, g_pre_mix, w_in, b_forget, g_q, g_k, g_sgu, b_sgu, w_spatial, b_spatial,
           w_branch_a, w_branch_b, w_out, g_post_mix, g_pre_ffn, w_ffn_in, w_ffn_down,
           g_post_ffn):
    bsz, s_len, _ = x.shape
    e, tri, pq, pk, oq, ok = _constants()
    for layer in range(g_pre_mix.shape[0]):
        w = w_in[layer]
        wqkv = w[:, Q_OFF:F_OFF].astype(BF16)
        wf = jnp.pad(jnp.tile(w[:, F_OFF:U_OFF], (1, N_SPLIT)),
                     ((0, 0), (0, LANES - N_SPLIT * HEADS))).astype(BF16)
        bf = jnp.pad(jnp.tile(b_forget[layer], N_SPLIT), (0, LANES - N_SPLIT * HEADS))[None, :]
        wuv = w[:, U_OFF:G_OFF].astype(BF16)
        wg = w[:, G_OFF:].astype(BF16)
        gq = jnp.tile(g_q[layer], HEADS)[None, :]
        gk = jnp.tile(g_k[layer], HEADS)[None, :]
        bsp = jnp.repeat(jnp.transpose(b_spatial[layer]), SGU_WIDTH // SGU_GROUPS, axis=1)

        qT, k, vT = _qkv_call(x, g_pre_mix[layer][None, :], wqkv, wf, bf, gq, gk, e, tri,
                              pq, pk, oq, ok)
        aT = _attn_call(qT, k, vT)
        x = _mix_call(x, aT, g_pre_mix[layer][None, :], wuv, wg, g_sgu[layer][None, :],
                      b_sgu[layer][None, :], w_spatial[layer], bsp,
                      w_branch_a[layer].astype(BF16), w_branch_b[layer].astype(BF16),
                      w_out[layer].astype(BF16), g_post_mix[layer][None, :])
        x = _ffn_call(x.reshape(bsz * s_len, D_MODEL), g_pre_ffn[layer][None, :],
                      w_ffn_in[layer].astype(BF16), w_ffn_down[layer].astype(BF16),
                      g_post_ffn[layer][None, :]).reshape(bsz, s_len, D_MODEL)
    return x
```

```python
import functools

import numpy as np
import jax
import jax.numpy as jnp
from jax import lax
from jax.experimental import pallas as pl
from jax.experimental.pallas import tpu as pltpu

F32 = jnp.float32
BF16 = jnp.bfloat16

D_MODEL = 1024
CHUNK = 64
HEAD_DIM = 64
FOX_WIDTH = 512
HEADS = 8
SGU_WIDTH = 512
SGU_GROUPS = 8
SGU_WINDOW = 128
D_FF = 2816
EPS = 1e-6

Q_OFF = 0
K_OFF = 512
V_OFF = 1024
F_OFF = 1536
U_OFF = 1544
G_OFF = 2568

LANES = 128
HEAD_PAD = LANES
N_SPLIT = 3
AUG_A = HEAD_DIM
AUG_B = HEAD_DIM + N_SPLIT

TM_QKV = 512
TQ = TM_QKV
TK = TM_QKV
N_Q = 4096 // TQ
LAG2 = 2
LAG3 = 4
RING = 3
V_ROWS = HEAD_DIM + 16
LOG2E = 1.4426950408889634
TM_MIX = 256
TM_FFN = 256
FF_CHUNK = 1408
NEG = -1e30

VMEM_LIMIT = 56 * 1024 * 1024


def _rms(x, g):
    ms = jnp.mean(x * x, axis=-1, keepdims=True)
    return x * lax.rsqrt(ms + EPS) * g


def _split3(d):
    d1 = d.astype(BF16)
    r1 = d - d1.astype(F32)
    d2 = r1.astype(BF16)
    d3 = (r1 - d2.astype(F32)).astype(BF16)
    return d1, d2, d3


def _qkv_kernel(x_ref, gpre_ref, wqkv_ref, wf_ref, bf_ref, gq_ref, gk_ref, e_ref, pk_ref, ok_ref,
                qT_ref, k_ref, vT_ref, carry_ref):
    @pl.when(pl.program_id(1) == 0)
    def _():
        carry_ref[...] = jnp.zeros_like(carry_ref)

    x = x_ref[0]
    h = _rms(x, gpre_ref[...]).astype(BF16)

    f = lax.dot_general(wf_ref[...], h, (((1,), (1,)), ((), ())),
                        preferred_element_type=F32)[:HEADS, :]
    f = f + jnp.concatenate([bf_ref[...]] * (TM_QKV // LANES), axis=1)
    dT = jnp.minimum(f, 0.0) - jnp.log1p(jnp.exp(-jnp.abs(f)))
    lane = lax.broadcasted_iota(jnp.int32, dT.shape, 1)
    shift = 1
    while shift < TM_QKV:
        dT = dT + jnp.where(lane >= shift, pltpu.roll(dT, shift, 1), 0.0)
        shift *= 2
    dT = dT + jnp.concatenate([carry_ref[...]] * (TM_QKV // LANES), axis=1)
    carry_ref[...] = jnp.broadcast_to(dT[:, TM_QKV - 1:TM_QKV], carry_ref.shape)
    d1, d2, d3 = (p.astype(F32) for p in _split3(dT * LOG2E))

    qT = jnp.dot(h, wqkv_ref[:, Q_OFF:K_OFF], preferred_element_type=F32).T
    gq = jnp.concatenate([gq_ref[...] * (HEAD_DIM ** -0.5 * LOG2E)] * (TM_QKV // LANES), axis=1)
    sub = lax.broadcasted_iota(jnp.int32, (8, TM_QKV), 0)
    pad = jnp.zeros((HEAD_PAD - HEAD_DIM - 8, TM_QKV), F32)
    for hd in range(HEADS):
        rows = qT[hd * HEAD_DIM:(hd + 1) * HEAD_DIM, :]
        ms = jnp.mean(rows * rows, axis=0, keepdims=True)
        qn = rows * lax.rsqrt(ms + EPS) * gq[hd * HEAD_DIM:(hd + 1) * HEAD_DIM, :]
        aug = jnp.where(sub == 0, d1[hd:hd + 1], jnp.where(sub == 1, d2[hd:hd + 1], jnp.where(
            sub == 2, d3[hd:hd + 1], jnp.where(sub < 2 * N_SPLIT, 1.0, 0.0))))
        qT_ref[0, hd, 0] = jnp.concatenate([qn, aug, pad], axis=0).astype(BF16)

    kk = jnp.dot(h, wqkv_ref[:, K_OFF:V_OFF], preferred_element_type=F32)
    ms = jnp.dot((kk * kk).astype(BF16), e_ref[...], preferred_element_type=F32)
    kn = kk * lax.rsqrt(ms + EPS) * gk_ref[...]
    zrows = jnp.zeros((LANES - N_SPLIT * HEADS, TM_QKV), F32)
    dsel = jnp.concatenate([d1, d2, d3, zrows], axis=0).T.astype(BF16)
    aug_k = jnp.dot(dsel, pk_ref[...], preferred_element_type=F32) + ok_ref[...]
    lo = lax.broadcasted_iota(jnp.int32, (TM_QKV, LANES), 1) < HEAD_DIM
    for i in range(HEADS // 2):
        t = kn[:, i * LANES:(i + 1) * LANES]
        for par in range(2):
            hs = slice((2 * i + par) * HEAD_PAD, (2 * i + par + 1) * HEAD_PAD)
            src = t if par == 0 else pltpu.roll(t, HEAD_DIM, 1)
            k_ref[0, :, hs] = jnp.where(lo, src, aug_k[:, hs]).astype(BF16)

    vT = jnp.dot(h, wqkv_ref[:, V_OFF:F_OFF], preferred_element_type=F32).T
    ones_rows = (lax.broadcasted_iota(jnp.int32, (V_ROWS - HEAD_DIM, TM_QKV), 0) == 0).astype(BF16)
    for hd in range(HEADS):
        vT_ref[0, hd, 0, :HEAD_DIM, :] = vT[hd * HEAD_DIM:(hd + 1) * HEAD_DIM, :].astype(BF16)
        vT_ref[0, hd, 0, HEAD_DIM:, :] = ones_rows


def _qkv_call(x, gpre, wqkv, wf, bf, gq, gk, e, pk, ok):
    b, s, _ = x.shape
    n_s = s // TM_QKV
    const = lambda shape: pl.BlockSpec(shape, lambda bi, si: (0,) * len(shape))
    return pl.pallas_call(
        _qkv_kernel,
        grid=(b, n_s),
        in_specs=[
            pl.BlockSpec((1, TM_QKV, D_MODEL), lambda bi, si: (bi, si, 0)),
            const((1, D_MODEL)),
            const((D_MODEL, 3 * FOX_WIDTH)),
            const((2 * HEADS, D_MODEL)),
            const((HEADS, LANES)),
            const((FOX_WIDTH, LANES)),
            const((1, FOX_WIDTH)),
            const((FOX_WIDTH, FOX_WIDTH)),
            const((LANES, HEADS * HEAD_PAD)),
            const((1, HEADS * HEAD_PAD)),
        ],
        out_specs=[
            pl.BlockSpec((1, HEADS, 1, HEAD_PAD, TM_QKV), lambda bi, si: (bi, 0, si, 0, 0)),
            pl.BlockSpec((1, TM_QKV, HEADS * HEAD_PAD), lambda bi, si: (bi, si, 0)),
            pl.BlockSpec((1, HEADS, 1, V_ROWS, TM_QKV), lambda bi, si: (bi, 0, si, 0, 0)),
        ],
        out_shape=[
            jax.ShapeDtypeStruct((b, HEADS, n_s, HEAD_PAD, TM_QKV), BF16),
            jax.ShapeDtypeStruct((b, s, HEADS * HEAD_PAD), BF16),
            jax.ShapeDtypeStruct((b, HEADS, n_s, V_ROWS, TM_QKV), BF16),
        ],
        scratch_shapes=[pltpu.VMEM((HEADS, LANES), F32)],
        compiler_params=pltpu.CompilerParams(
            dimension_semantics=("arbitrary", "arbitrary"), vmem_limit_bytes=VMEM_LIMIT),
        name="fox_qkv",
    )(x, gpre, wqkv, wf, bf, gq, gk, e, pk, ok)


def _attn_kernel(qT_ref, k_ref, vT_ref, o_ref, *scratch):
    sb = scratch[:RING]
    pb = scratch[RING:2 * RING]
    m_ref, acc_ref, bias_ref = scratch[2 * RING:]

    @pl.when((pl.program_id(0) == 0) & (pl.program_id(1) == 0))
    def _():
        row = lax.broadcasted_iota(jnp.int32, (TK, TQ), 0)
        col = lax.broadcasted_iota(jnp.int32, (TK, TQ), 1)
        bias_ref[...] = jnp.where(row <= col, 0.0, NEG)

    def s1(t, r, diag):
        qi, j = t
        start = j * TK if isinstance(j, int) else pl.multiple_of(j * TK, TK)
        kj = k_ref[0, pl.ds(start, TK), :]
        s = jnp.dot(kj, qT_ref[0, 0, qi], preferred_element_type=F32)
        if diag:
            s = s + bias_ref[...]
        sb[r][...] = s
        cmax = jnp.max(s, axis=0, keepdims=True)
        if diag:
            m_before = jnp.full((1, TQ), NEG, F32)
            m_after = cmax
        else:
            m_before = m_ref[qi]
            m_after = jnp.maximum(m_before, cmax)
        m_ref[qi] = m_after
        return m_before, m_after

    def s2(r, ms):
        m_before, m_after = ms
        pb[r][...] = jnp.exp2(sb[r][...] - m_after).astype(BF16)
        return jnp.exp2(m_before - m_after)

    def s3(t, r, a, diag):
        qi, j = t
        pv = jnp.dot(vT_ref[0, 0, j], pb[r][...], preferred_element_type=F32)
        if diag:
            acc_ref[qi] = pv
        else:
            acc_ref[qi] = a * acc_ref[qi] + pv

    def pipeline(tiles, diag, successor):
        n_tiles = len(tiles)
        ms = {}
        resc = {}

        def slot(u):
            n3 = u - LAG3
            if 0 <= n3 < n_tiles:
                s3(tiles[n3], n3 % RING, resc.pop(n3), diag)
            n2 = u - LAG2
            if 0 <= n2 < n_tiles:
                resc[n2] = s2(n2 % RING, ms.pop(n2))
            if u < n_tiles:
                ms[u] = s1(tiles[u], u % RING, diag)

        loop_lo = LAG3
        n_iter = (n_tiles - loop_lo) // RING
        loop_hi = loop_lo + n_iter * RING
        for u in range(loop_lo):
            slot(u)

        def body(_, carry):
            hist, ms_q, resc_q = carry
            for i in range(RING):
                hist = hist[1:] + (successor(hist[-1]),)
                u = loop_lo + i
                s3(hist[0], (u - LAG3) % RING, resc_q[0], diag)
                resc_q = resc_q[1:] + (s2((u - LAG2) % RING, ms_q[0]),)
                ms_q = ms_q[1:] + (s1(hist[-1], u % RING, diag),)
            return hist, ms_q, resc_q

        if n_iter > 0:
            first = (tiles[0],) + tuple(tiles[:loop_lo])
            hist0 = tuple((jnp.int32(q), jnp.int32(j)) for q, j in first)
            ms0 = tuple(ms.pop(n) for n in range(loop_lo - LAG2, loop_lo))
            resc0 = tuple(resc.pop(n) for n in range(loop_lo - LAG3, loop_lo - LAG2))
            _, ms1, resc1 = lax.fori_loop(0, n_iter, body, (hist0, ms0, resc0))
            ms.update(zip(range(loop_hi - LAG2, loop_hi), ms1))
            resc.update(zip(range(loop_hi - LAG3, loop_hi - LAG2), resc1))
        for u in range(loop_hi, n_tiles + LAG3):
            slot(u)

    def next_full(t):
        qi, j = t
        wrap = j + 1 == qi
        return jnp.where(wrap, qi + 1, qi), jnp.where(wrap, 0, j + 1)

    pipeline([(q, q) for q in range(N_Q)], True, lambda t: (t[0] + 1, t[1] + 1))
    pipeline([(q, j) for q in range(1, N_Q) for j in range(q)], False, next_full)

    for qi in range(N_Q):
        acc = acc_ref[qi]
        o_ref[0, 0, qi] = (acc[:HEAD_DIM] / acc[HEAD_DIM:HEAD_DIM + 1]).astype(BF16)


def _attn_call(qT, k, vT):
    b = qT.shape[0]
    s = k.shape[1]
    return pl.pallas_call(
        _attn_kernel,
        grid=(b, HEADS),
        in_specs=[
            pl.BlockSpec((1, 1, N_Q, HEAD_PAD, TQ), lambda bi, hi: (bi, hi, 0, 0, 0)),
            pl.BlockSpec((1, s, HEAD_PAD), lambda bi, hi: (bi, 0, hi)),
            pl.BlockSpec((1, 1, N_Q, V_ROWS, TK), lambda bi, hi: (bi, hi, 0, 0, 0)),
        ],
        out_specs=pl.BlockSpec((1, 1, N_Q, HEAD_DIM, TQ), lambda bi, hi: (bi, hi, 0, 0, 0)),
        out_shape=jax.ShapeDtypeStruct((b, HEADS, N_Q, HEAD_DIM, TQ), BF16),
        scratch_shapes=(
            [pltpu.VMEM((TK, TQ), F32)] * RING + [pltpu.VMEM((TK, TQ), BF16)] * RING
            + [pltpu.VMEM((N_Q, 1, TQ), F32), pltpu.VMEM((N_Q, V_ROWS, TQ), F32),
               pltpu.VMEM((TK, TQ), F32)]),
        compiler_params=pltpu.CompilerParams(
            dimension_semantics=("arbitrary", "arbitrary"),
            vmem_limit_bytes=VMEM_LIMIT),
        name="fox_attn",
    )(qT, k, vT)


def _mix_kernel(x_ref, aT_ref, gpre_ref, wuv_ref, wg_ref, gsgu_ref, bsgu_ref, ws_ref, bsp_ref,
                wa_ref, wb_ref, wout_ref, gpost_ref, o_ref, mixed_ref):
    x = x_ref[0]
    h = _rms(x, gpre_ref[...]).astype(BF16)
    uv = jax.nn.gelu(jnp.dot(h, wuv_ref[...], preferred_element_type=F32))
    u = uv[:, :SGU_WIDTH]
    v = uv[:, SGU_WIDTH:]
    mu = jnp.mean(v, axis=-1, keepdims=True)
    vc = v - mu
    var = jnp.mean(vc * vc, axis=-1, keepdims=True)
    vn = vc * lax.rsqrt(var + EPS) * gsgu_ref[...] + bsgu_ref[...]

    t_idx = lax.broadcasted_iota(jnp.int32, (SGU_WINDOW, SGU_WINDOW), 0)
    s_idx = lax.broadcasted_iota(jnp.int32, (SGU_WINDOW, SGU_WINDOW), 1)
    wmask = (s_idx // CHUNK) <= (t_idx // CHUNK)
    lo = lax.broadcasted_iota(jnp.int32, (SGU_WINDOW, LANES), 1) < (SGU_WIDTH // SGU_GROUPS)
    for gp in range(SGU_GROUPS // 2):
        w_pair = jnp.concatenate(
            [jnp.where(wmask, ws_ref[2 * gp], 0.0), jnp.where(wmask, ws_ref[2 * gp + 1], 0.0)],
            axis=1).astype(BF16)
        for w in range(TM_MIX // SGU_WINDOW):
            vp = vn[w * SGU_WINDOW:(w + 1) * SGU_WINDOW, gp * LANES:(gp + 1) * LANES]
            rhs = jnp.concatenate([jnp.where(lo, vp, 0.0), jnp.where(lo, 0.0, vp)],
                                  axis=0).astype(BF16)
            mixed_ref[w * SGU_WINDOW:(w + 1) * SGU_WINDOW, gp * LANES:(gp + 1) * LANES] = (
                jnp.dot(w_pair, rhs, preferred_element_type=F32)
                + bsp_ref[:, gp * LANES:(gp + 1) * LANES])
    sgu = (u * mixed_ref[...]).astype(BF16)

    y_b = jnp.dot(sgu, wb_ref[...], preferred_element_type=F32)
    aT = aT_ref[0, :, 0].reshape(FOX_WIDTH, TM_MIX)
    y_a = lax.dot_general(aT, wa_ref[...], (((0,), (0,)), ((), ())),
                          preferred_element_type=F32)
    gates = jax.nn.sigmoid(jnp.dot(h, wg_ref[...], preferred_element_type=F32))
    merged = (gates[:, :D_MODEL] * y_a + gates[:, D_MODEL:] * y_b).astype(BF16)
    o = jnp.dot(merged, wout_ref[...], preferred_element_type=F32)
    o_ref[0] = x + _rms(o, gpost_ref[...])


def _mix_call(x, aT, gpre, wuv, wg, gsgu, bsgu, ws, bsp, wa, wb, wout, gpost):
    b, s, _ = x.shape
    const = lambda shape: pl.BlockSpec(shape, lambda bi, si: (0,) * len(shape))
    return pl.pallas_call(
        _mix_kernel,
        grid=(b, s // TM_MIX),
        in_specs=[
            pl.BlockSpec((1, TM_MIX, D_MODEL), lambda bi, si: (bi, si, 0)),
            pl.BlockSpec((1, HEADS, 1, HEAD_DIM, TM_MIX),
                         lambda bi, si: (bi, 0, si // (TQ // TM_MIX), 0, si % (TQ // TM_MIX))),
            const((1, D_MODEL)),
            const((D_MODEL, 2 * SGU_WIDTH)),
            const((D_MODEL, 2 * D_MODEL)),
            const((1, SGU_WIDTH)),
            const((1, SGU_WIDTH)),
            const((SGU_GROUPS, SGU_WINDOW, SGU_WINDOW)),
            const((SGU_WINDOW, SGU_WIDTH)),
            const((FOX_WIDTH, D_MODEL)),
            const((SGU_WIDTH, D_MODEL)),
            const((D_MODEL, D_MODEL)),
            const((1, D_MODEL)),
        ],
        out_specs=pl.BlockSpec((1, TM_MIX, D_MODEL), lambda bi, si: (bi, si, 0)),
        out_shape=jax.ShapeDtypeStruct(x.shape, F32),
        scratch_shapes=[pltpu.VMEM((TM_MIX, SGU_WIDTH), F32)],
        compiler_params=pltpu.CompilerParams(
            dimension_semantics=("arbitrary", "arbitrary"), vmem_limit_bytes=VMEM_LIMIT),
        name="fox_mix",
    )(x, aT, gpre, wuv, wg, gsgu, bsgu, ws, bsp, wa, wb, wout, gpost)


def _ffn_kernel(x_ref, gpre_ref, win_ref, wdown_ref, gpost_ref, o_ref):
    x = x_ref[...]
    h = _rms(x, gpre_ref[...]).astype(BF16)
    ff = jnp.zeros((TM_FFN, D_MODEL), F32)
    for c in range(D_FF // FF_CHUNK):
        g = jnp.dot(h, win_ref[:, c * FF_CHUNK:(c + 1) * FF_CHUNK], preferred_element_type=F32)
        u = jnp.dot(h, win_ref[:, D_FF + c * FF_CHUNK:D_FF + (c + 1) * FF_CHUNK],
                    preferred_element_type=F32)
        a = (jax.nn.silu(g) * u).astype(BF16)
        ff = ff + jnp.dot(a, wdown_ref[c * FF_CHUNK:(c + 1) * FF_CHUNK, :],
                          preferred_element_type=F32)
    o_ref[...] = x + _rms(ff, gpost_ref[...])


def _ffn_call(x, gpre, win, wdown, gpost):
    n, _ = x.shape
    const = lambda shape: pl.BlockSpec(shape, lambda i: (0,) * len(shape))
    return pl.pallas_call(
        _ffn_kernel,
        grid=(n // TM_FFN,),
        in_specs=[
            pl.BlockSpec((TM_FFN, D_MODEL), lambda i: (i, 0)),
            const((1, D_MODEL)),
            const((D_MODEL, 2 * D_FF)),
            const((D_FF, D_MODEL)),
            const((1, D_MODEL)),
        ],
        out_specs=pl.BlockSpec((TM_FFN, D_MODEL), lambda i: (i, 0)),
        out_shape=jax.ShapeDtypeStruct(x.shape, F32),
        compiler_params=pltpu.CompilerParams(
            dimension_semantics=("arbitrary",), vmem_limit_bytes=VMEM_LIMIT),
        name="fox_ffn",
    )(x, gpre, win, wdown, gpost)


def _constants():
    e = np.kron(np.eye(HEADS, dtype=np.float32),
                np.full((HEAD_DIM, HEAD_DIM), 1.0 / HEAD_DIM, np.float32))
    pk = np.zeros((LANES, HEADS * HEAD_PAD), np.float32)
    ok = np.zeros((1, HEADS * HEAD_PAD), np.float32)
    for hd in range(HEADS):
        for j in range(N_SPLIT):
            pk[j * HEADS + hd, hd * HEAD_PAD + AUG_B + j] = -1.0
            ok[0, hd * HEAD_PAD + AUG_A + j] = 1.0
    return jnp.asarray(e, BF16), jnp.asarray(pk, BF16), jnp.asarray(ok)


def _qkv_stage(x, g_pre, w, b_forget, g_q, g_k):
    e, pk, ok = _constants()
    wqkv = w[:, Q_OFF:F_OFF].astype(BF16)
    wf = jnp.pad(w[:, F_OFF:U_OFF].T, ((0, HEADS), (0, 0))).astype(BF16)
    bf = jnp.broadcast_to(b_forget[:, None], (HEADS, LANES))
    gq = jnp.broadcast_to(jnp.tile(g_q, HEADS)[:, None], (FOX_WIDTH, LANES))
    gk = jnp.tile(g_k, HEADS)[None, :]
    return _qkv_call(x, g_pre[None, :], wqkv, wf, bf, gq, gk, e, pk, ok)


def kernel(x, g_pre_mix, w_in, b_forget, g_q, g_k, g_sgu, b_sgu, w_spatial, b_spatial,
           w_branch_a, w_branch_b, w_out, g_post_mix, g_pre_ffn, w_ffn_in, w_ffn_down,
           g_post_ffn):
    bsz, s_len, _ = x.shape
    for layer in range(g_pre_mix.shape[0]):
        w = w_in[layer]
        wuv = w[:, U_OFF:G_OFF].astype(BF16)
        wg = w[:, G_OFF:].astype(BF16)
        bsp = jnp.repeat(jnp.transpose(b_spatial[layer]), SGU_WIDTH // SGU_GROUPS, axis=1)

        qT, k, vT = _qkv_stage(x, g_pre_mix[layer], w, b_forget[layer], g_q[layer], g_k[layer])
        aT = _attn_call(qT, k, vT)
        x = _mix_call(x, aT, g_pre_mix[layer][None, :], wuv, wg, g_sgu[layer][None, :],
                      b_sgu[layer][None, :], w_spatial[layer], bsp,
                      w_branch_a[layer].astype(BF16), w_branch_b[layer].astype(BF16),
                      w_out[layer].astype(BF16), g_post_mix[layer][None, :])
        x = _ffn_call(x.reshape(bsz * s_len, D_MODEL), g_pre_ffn[layer][None, :],
                      w_ffn_in[layer].astype(BF16), w_ffn_down[layer].astype(BF16),
                      g_post_ffn[layer][None, :]).reshape(bsz, s_len, D_MODEL)
    return x
```

```python
import functools

import numpy as np
import jax
import jax.numpy as jnp
from jax import lax
from jax.experimental import pallas as pl
from jax.experimental.pallas import tpu as pltpu

F32 = jnp.float32
BF16 = jnp.bfloat16

D_MODEL = 1024
CHUNK = 64
HEAD_DIM = 64
FOX_WIDTH = 512
HEADS = 8
SGU_WIDTH = 512
SGU_GROUPS = 8
SGU_WINDOW = 128
D_FF = 2816
EPS = 1e-6

Q_OFF = 0
K_OFF = 512
V_OFF = 1024
F_OFF = 1536
U_OFF = 1544
G_OFF = 2568

LANES = 128
HEAD_PAD = LANES
N_SPLIT = 3
AUG_A = HEAD_DIM
AUG_B = HEAD_DIM + N_SPLIT

TM_QKV = 512
TQ = TM_QKV
TK = TM_QKV
N_Q = 4096 // TQ
LAG2 = 2
LAG3 = 4
RING = 3
V_ROWS = HEAD_DIM + 16
LOG2E = 1.4426950408889634
TM_MIX = 512
TM_FFN = 512
FF_CHUNK = 1408
NEG = -1e30

VMEM_LIMIT = 56 * 1024 * 1024


def _rms(x, g):
    ms = jnp.mean(x * x, axis=-1, keepdims=True)
    return x * lax.rsqrt(ms + EPS) * g


def _split3(d):
    d1 = d.astype(BF16)
    r1 = d - d1.astype(F32)
    d2 = r1.astype(BF16)
    d3 = (r1 - d2.astype(F32)).astype(BF16)
    return d1, d2, d3


def _win_prep_kernel(wT_ref, wqkv_ref, wfT_ref, wug_ref):
    wqkv_ref[...] = wT_ref[Q_OFF:F_OFF, :].T.astype(BF16)
    wug_ref[...] = wT_ref[U_OFF:, :].T.astype(BF16)
    wfT_ref[...] = wT_ref[F_OFF:F_OFF + 2 * HEADS, :].astype(BF16)


def _win_prep_call(wT):
    cols, d = wT.shape
    chunk = d // 4
    return pl.pallas_call(
        _win_prep_kernel,
        grid=(d // chunk,),
        in_specs=[pl.BlockSpec((cols, chunk), lambda i: (0, i))],
        out_specs=[pl.BlockSpec((chunk, F_OFF), lambda i: (i, 0)),
                   pl.BlockSpec((2 * HEADS, chunk), lambda i: (0, i)),
                   pl.BlockSpec((chunk, cols - U_OFF), lambda i: (i, 0))],
        out_shape=[jax.ShapeDtypeStruct((d, F_OFF), BF16),
                   jax.ShapeDtypeStruct((2 * HEADS, d), BF16),
                   jax.ShapeDtypeStruct((d, cols - U_OFF), BF16)],
        compiler_params=pltpu.CompilerParams(
            dimension_semantics=("arbitrary",), vmem_limit_bytes=VMEM_LIMIT),
        name="fox_win_prep",
    )(wT)


def _qkv_kernel(x_ref, gpre_ref, wqkv_ref, wf_ref, bf_ref, gq_ref, gk_ref, e_ref, pk_ref, ok_ref,
                qT_ref, k_ref, vT_ref, carry_ref):
    @pl.when(pl.program_id(1) == 0)
    def _():
        carry_ref[...] = jnp.zeros_like(carry_ref)

    x = x_ref[0]
    h = _rms(x, gpre_ref[...]).astype(BF16)

    f = lax.dot_general(wf_ref[...], h, (((1,), (1,)), ((), ())),
                        preferred_element_type=F32)[:HEADS, :]
    f = f + jnp.concatenate([bf_ref[...]] * (TM_QKV // LANES), axis=1)
    dT = jnp.minimum(f, 0.0) - jnp.log1p(jnp.exp(-jnp.abs(f)))
    lane = lax.broadcasted_iota(jnp.int32, dT.shape, 1)
    shift = 1
    while shift < TM_QKV:
        dT = dT + jnp.where(lane >= shift, pltpu.roll(dT, shift, 1), 0.0)
        shift *= 2
    dT = dT + jnp.concatenate([carry_ref[...]] * (TM_QKV // LANES), axis=1)
    carry_ref[...] = jnp.broadcast_to(dT[:, TM_QKV - 1:TM_QKV], carry_ref.shape)
    d1, d2, d3 = (p.astype(F32) for p in _split3(dT * LOG2E))

    qT = jnp.dot(h, wqkv_ref[:, Q_OFF:K_OFF], preferred_element_type=F32).T
    gq = jnp.concatenate([gq_ref[...] * (HEAD_DIM ** -0.5 * LOG2E)] * (TM_QKV // LANES), axis=1)
    sub = lax.broadcasted_iota(jnp.int32, (8, TM_QKV), 0)
    pad = jnp.zeros((HEAD_PAD - HEAD_DIM - 8, TM_QKV), F32)
    for hd in range(HEADS):
        rows = qT[hd * HEAD_DIM:(hd + 1) * HEAD_DIM, :]
        ms = jnp.mean(rows * rows, axis=0, keepdims=True)
        qn = rows * lax.rsqrt(ms + EPS) * gq[hd * HEAD_DIM:(hd + 1) * HEAD_DIM, :]
        aug = jnp.where(sub == 0, d1[hd:hd + 1], jnp.where(sub == 1, d2[hd:hd + 1], jnp.where(
            sub == 2, d3[hd:hd + 1], jnp.where(sub < 2 * N_SPLIT, 1.0, 0.0))))
        qT_ref[0, hd, 0] = jnp.concatenate([qn, aug, pad], axis=0).astype(BF16)

    kk = jnp.dot(h, wqkv_ref[:, K_OFF:V_OFF], preferred_element_type=F32)
    ms = jnp.dot((kk * kk).astype(BF16), e_ref[...], preferred_element_type=F32)
    kn = kk * lax.rsqrt(ms + EPS) * gk_ref[...]
    zrows = jnp.zeros((LANES - N_SPLIT * HEADS, TM_QKV), F32)
    dsel = jnp.concatenate([d1, d2, d3, zrows], axis=0).T.astype(BF16)
    aug_k = jnp.dot(dsel, pk_ref[...], preferred_element_type=F32) + ok_ref[...]
    lo = lax.broadcasted_iota(jnp.int32, (TM_QKV, LANES), 1) < HEAD_DIM
    for i in range(HEADS // 2):
        t = kn[:, i * LANES:(i + 1) * LANES]
        for par in range(2):
            hs = slice((2 * i + par) * HEAD_PAD, (2 * i + par + 1) * HEAD_PAD)
            src = t if par == 0 else pltpu.roll(t, HEAD_DIM, 1)
            k_ref[0, :, hs] = jnp.where(lo, src, aug_k[:, hs]).astype(BF16)

    vT = jnp.dot(h, wqkv_ref[:, V_OFF:F_OFF], preferred_element_type=F32).T
    ones_rows = (lax.broadcasted_iota(jnp.int32, (V_ROWS - HEAD_DIM, TM_QKV), 0) == 0).astype(BF16)
    for hd in range(HEADS):
        vT_ref[0, hd, 0, :HEAD_DIM, :] = vT[hd * HEAD_DIM:(hd + 1) * HEAD_DIM, :].astype(BF16)
        vT_ref[0, hd, 0, HEAD_DIM:, :] = ones_rows


def _qkv_call(x, gpre, wqkv, wf, bf, gq, gk, e, pk, ok):
    b, s, _ = x.shape
    n_s = s // TM_QKV
    const = lambda shape: pl.BlockSpec(shape, lambda bi, si: (0,) * len(shape))
    return pl.pallas_call(
        _qkv_kernel,
        grid=(b, n_s),
        in_specs=[
            pl.BlockSpec((1, TM_QKV, D_MODEL), lambda bi, si: (bi, si, 0)),
            const((1, D_MODEL)),
            const((D_MODEL, 3 * FOX_WIDTH)),
            const((2 * HEADS, D_MODEL)),
            const((HEADS, LANES)),
            const((FOX_WIDTH, LANES)),
            const((1, FOX_WIDTH)),
            const((FOX_WIDTH, FOX_WIDTH)),
            const((LANES, HEADS * HEAD_PAD)),
            const((1, HEADS * HEAD_PAD)),
        ],
        out_specs=[
            pl.BlockSpec((1, HEADS, 1, HEAD_PAD, TM_QKV), lambda bi, si: (bi, 0, si, 0, 0)),
            pl.BlockSpec((1, TM_QKV, HEADS * HEAD_PAD), lambda bi, si: (bi, si, 0)),
            pl.BlockSpec((1, HEADS, 1, V_ROWS, TM_QKV), lambda bi, si: (bi, 0, si, 0, 0)),
        ],
        out_shape=[
            jax.ShapeDtypeStruct((b, HEADS, n_s, HEAD_PAD, TM_QKV), BF16),
            jax.ShapeDtypeStruct((b, s, HEADS * HEAD_PAD), BF16),
            jax.ShapeDtypeStruct((b, HEADS, n_s, V_ROWS, TM_QKV), BF16),
        ],
        scratch_shapes=[pltpu.VMEM((HEADS, LANES), F32)],
        compiler_params=pltpu.CompilerParams(
            dimension_semantics=("arbitrary", "arbitrary"), vmem_limit_bytes=VMEM_LIMIT),
        name="fox_qkv",
    )(x, gpre, wqkv, wf, bf, gq, gk, e, pk, ok)


def _attn_kernel(qT_ref, k_ref, vT_ref, o_ref, *scratch):
    sb = scratch[:RING]
    pb = scratch[RING:2 * RING]
    m_ref, acc_ref, bias_ref = scratch[2 * RING:]

    @pl.when((pl.program_id(0) == 0) & (pl.program_id(1) == 0))
    def _():
        row = lax.broadcasted_iota(jnp.int32, (TK, TQ), 0)
        col = lax.broadcasted_iota(jnp.int32, (TK, TQ), 1)
        bias_ref[...] = jnp.where(row <= col, 0.0, NEG)

    def s1(t, r, diag):
        qi, j = t
        start = j * TK if isinstance(j, int) else pl.multiple_of(j * TK, TK)
        kj = k_ref[0, pl.ds(start, TK), :]
        s = jnp.dot(kj, qT_ref[0, 0, qi], preferred_element_type=F32)
        if diag:
            s = s + bias_ref[...]
        sb[r][...] = s
        cmax = jnp.max(s, axis=0, keepdims=True)
        if diag:
            m_before = jnp.full((1, TQ), NEG, F32)
            m_after = cmax
        else:
            m_before = m_ref[qi]
            m_after = jnp.maximum(m_before, cmax)
        m_ref[qi] = m_after
        return m_before, m_after

    def s2(r, ms):
        m_before, m_after = ms
        pb[r][...] = jnp.exp2(sb[r][...] - m_after).astype(BF16)
        return jnp.exp2(m_before - m_after)

    def s3(t, r, a, diag):
        qi, j = t
        pv = jnp.dot(vT_ref[0, 0, j], pb[r][...], preferred_element_type=F32)
        if diag:
            acc_ref[qi] = pv
        else:
            acc_ref[qi] = a * acc_ref[qi] + pv

    def pipeline(tiles, diag, successor):
        n_tiles = len(tiles)
        ms = {}
        resc = {}

        def slot(u):
            n3 = u - LAG3
            if 0 <= n3 < n_tiles:
                s3(tiles[n3], n3 % RING, resc.pop(n3), diag)
            n2 = u - LAG2
            if 0 <= n2 < n_tiles:
                resc[n2] = s2(n2 % RING, ms.pop(n2))
            if u < n_tiles:
                ms[u] = s1(tiles[u], u % RING, diag)

        loop_lo = LAG3
        n_iter = (n_tiles - loop_lo) // RING
        loop_hi = loop_lo + n_iter * RING
        for u in range(loop_lo):
            slot(u)

        def body(_, carry):
            hist, ms_q, resc_q = carry
            for i in range(RING):
                hist = hist[1:] + (successor(hist[-1]),)
                u = loop_lo + i
                s3(hist[0], (u - LAG3) % RING, resc_q[0], diag)
                resc_q = resc_q[1:] + (s2((u - LAG2) % RING, ms_q[0]),)
                ms_q = ms_q[1:] + (s1(hist[-1], u % RING, diag),)
            return hist, ms_q, resc_q

        if n_iter > 0:
            first = (tiles[0],) + tuple(tiles[:loop_lo])
            hist0 = tuple((jnp.int32(q), jnp.int32(j)) for q, j in first)
            ms0 = tuple(ms.pop(n) for n in range(loop_lo - LAG2, loop_lo))
            resc0 = tuple(resc.pop(n) for n in range(loop_lo - LAG3, loop_lo - LAG2))
            _, ms1, resc1 = lax.fori_loop(0, n_iter, body, (hist0, ms0, resc0))
            ms.update(zip(range(loop_hi - LAG2, loop_hi), ms1))
            resc.update(zip(range(loop_hi - LAG3, loop_hi - LAG2), resc1))
        for u in range(loop_hi, n_tiles + LAG3):
            slot(u)

    def next_full(t):
        qi, j = t
        wrap = j + 1 == qi
        return jnp.where(wrap, qi + 1, qi), jnp.where(wrap, 0, j + 1)

    pipeline([(q, q) for q in range(N_Q)], True, lambda t: (t[0] + 1, t[1] + 1))
    pipeline([(q, j) for q in range(1, N_Q) for j in range(q)], False, next_full)

    for qi in range(N_Q):
        acc = acc_ref[qi]
        o_ref[0, 0, qi] = (acc[:HEAD_DIM] / acc[HEAD_DIM:HEAD_DIM + 1]).astype(BF16)


def _attn_call(qT, k, vT):
    b = qT.shape[0]
    s = k.shape[1]
    return pl.pallas_call(
        _attn_kernel,
        grid=(b, HEADS),
        in_specs=[
            pl.BlockSpec((1, 1, N_Q, HEAD_PAD, TQ), lambda bi, hi: (bi, hi, 0, 0, 0)),
            pl.BlockSpec((1, s, HEAD_PAD), lambda bi, hi: (bi, 0, hi)),
            pl.BlockSpec((1, 1, N_Q, V_ROWS, TK), lambda bi, hi: (bi, hi, 0, 0, 0)),
        ],
        out_specs=pl.BlockSpec((1, 1, N_Q, HEAD_DIM, TQ), lambda bi, hi: (bi, hi, 0, 0, 0)),
        out_shape=jax.ShapeDtypeStruct((b, HEADS, N_Q, HEAD_DIM, TQ), BF16),
        scratch_shapes=(
            [pltpu.VMEM((TK, TQ), F32)] * RING + [pltpu.VMEM((TK, TQ), BF16)] * RING
            + [pltpu.VMEM((N_Q, 1, TQ), F32), pltpu.VMEM((N_Q, V_ROWS, TQ), F32),
               pltpu.VMEM((TK, TQ), F32)]),
        compiler_params=pltpu.CompilerParams(
            dimension_semantics=("arbitrary", "arbitrary"),
            vmem_limit_bytes=VMEM_LIMIT),
        name="fox_attn",
    )(qT, k, vT)


def _mix_kernel(x_ref, aT_ref, gpre_ref, wug_ref, gsgu_ref, bsgu_ref, ws_ref, bsp_ref,
                wa32_ref, wb32_ref, wout32_ref, gpost_ref, o_ref, mixed_ref,
                wa_ref, wb_ref, wout_ref):
    @pl.when((pl.program_id(0) == 0) & (pl.program_id(1) == 0))
    def _():
        wa_ref[...] = wa32_ref[0].astype(BF16)
        wb_ref[...] = wb32_ref[0].astype(BF16)
        wout_ref[...] = wout32_ref[0].astype(BF16)

    x = x_ref[0]
    h = _rms(x, gpre_ref[...]).astype(BF16)
    uv = jax.nn.gelu(jnp.dot(h, wug_ref[:, :2 * SGU_WIDTH], preferred_element_type=F32))
    u = uv[:, :SGU_WIDTH]
    v = uv[:, SGU_WIDTH:]
    mu = jnp.mean(v, axis=-1, keepdims=True)
    vc = v - mu
    var = jnp.mean(vc * vc, axis=-1, keepdims=True)
    vn = vc * lax.rsqrt(var + EPS) * gsgu_ref[...] + bsgu_ref[...]

    t_idx = lax.broadcasted_iota(jnp.int32, (SGU_WINDOW, SGU_WINDOW), 0)
    s_idx = lax.broadcasted_iota(jnp.int32, (SGU_WINDOW, SGU_WINDOW), 1)
    wmask = (s_idx // CHUNK) <= (t_idx // CHUNK)
    lo = lax.broadcasted_iota(jnp.int32, (SGU_WINDOW, LANES), 1) < (SGU_WIDTH // SGU_GROUPS)
    for gp in range(SGU_GROUPS // 2):
        w_pair = jnp.concatenate(
            [jnp.where(wmask, ws_ref[2 * gp], 0.0), jnp.where(wmask, ws_ref[2 * gp + 1], 0.0)],
            axis=1).astype(BF16)
        for w in range(TM_MIX // SGU_WINDOW):
            vp = vn[w * SGU_WINDOW:(w + 1) * SGU_WINDOW, gp * LANES:(gp + 1) * LANES]
            rhs = jnp.concatenate([jnp.where(lo, vp, 0.0), jnp.where(lo, 0.0, vp)],
                                  axis=0).astype(BF16)
            mixed_ref[w * SGU_WINDOW:(w + 1) * SGU_WINDOW, gp * LANES:(gp + 1) * LANES] = (
                jnp.dot(w_pair, rhs, preferred_element_type=F32)
                + bsp_ref[:, gp * LANES:(gp + 1) * LANES])
    sgu = (u * mixed_ref[...]).astype(BF16)

    y_b = jnp.dot(sgu, wb_ref[...], preferred_element_type=F32)
    aT = aT_ref[0, :, 0].reshape(FOX_WIDTH, TM_MIX)
    y_a = lax.dot_general(aT, wa_ref[...], (((0,), (0,)), ((), ())),
                          preferred_element_type=F32)
    gates = jax.nn.sigmoid(jnp.dot(h, wug_ref[:, 2 * SGU_WIDTH:], preferred_element_type=F32))
    merged = (gates[:, :D_MODEL] * y_a + gates[:, D_MODEL:] * y_b).astype(BF16)
    o = jnp.dot(merged, wout_ref[...], preferred_element_type=F32)
    o_ref[0] = x + _rms(o, gpost_ref[...])


def _mix_call(x, aT, gpre, wug, gsgu, bsgu, ws, bsp, wa, wb, wout, layer, gpost):
    b, s, _ = x.shape
    const = lambda shape: pl.BlockSpec(shape, lambda bi, si: (0,) * len(shape))
    layer_w = lambda shape: pl.BlockSpec((1,) + shape, lambda bi, si: (layer, 0, 0),
                                         pipeline_mode=pl.Buffered(1))
    return pl.pallas_call(
        _mix_kernel,
        grid=(b, s // TM_MIX),
        in_specs=[
            pl.BlockSpec((1, TM_MIX, D_MODEL), lambda bi, si: (bi, si, 0)),
            pl.BlockSpec((1, HEADS, 1, HEAD_DIM, TM_MIX),
                         lambda bi, si: (bi, 0, si // (TQ // TM_MIX), 0, si % (TQ // TM_MIX))),
            const((1, D_MODEL)),
            const((D_MODEL, 2 * SGU_WIDTH + 2 * D_MODEL)),
            const((1, SGU_WIDTH)),
            const((1, SGU_WIDTH)),
            const((SGU_GROUPS, SGU_WINDOW, SGU_WINDOW)),
            const((SGU_WINDOW, SGU_WIDTH)),
            layer_w((FOX_WIDTH, D_MODEL)),
            layer_w((SGU_WIDTH, D_MODEL)),
            layer_w((D_MODEL, D_MODEL)),
            const((1, D_MODEL)),
        ],
        out_specs=pl.BlockSpec((1, TM_MIX, D_MODEL), lambda bi, si: (bi, si, 0)),
        out_shape=jax.ShapeDtypeStruct(x.shape, F32),
        scratch_shapes=[pltpu.VMEM((TM_MIX, SGU_WIDTH), F32),
                        pltpu.VMEM((FOX_WIDTH, D_MODEL), BF16),
                        pltpu.VMEM((SGU_WIDTH, D_MODEL), BF16),
                        pltpu.VMEM((D_MODEL, D_MODEL), BF16)],
        compiler_params=pltpu.CompilerParams(
            dimension_semantics=("arbitrary", "arbitrary"), vmem_limit_bytes=VMEM_LIMIT),
        name="fox_mix",
    )(x, aT, gpre, wug, gsgu, bsgu, ws, bsp, wa, wb, wout, gpost)


def _ffn_kernel(x_ref, gpre_ref, win_ref, wdown_ref, gpost_ref, o_ref):
    x = x_ref[...]
    h = _rms(x, gpre_ref[...]).astype(BF16)
    ff = jnp.zeros((TM_FFN, D_MODEL), F32)
    for c in range(D_FF // FF_CHUNK):
        g = jnp.dot(h, win_ref[:, c * FF_CHUNK:(c + 1) * FF_CHUNK], preferred_element_type=F32)
        u = jnp.dot(h, win_ref[:, D_FF + c * FF_CHUNK:D_FF + (c + 1) * FF_CHUNK],
                    preferred_element_type=F32)
        a = (jax.nn.silu(g) * u).astype(BF16)
        ff = ff + jnp.dot(a, wdown_ref[c * FF_CHUNK:(c + 1) * FF_CHUNK, :],
                          preferred_element_type=F32)
    o_ref[...] = x + _rms(ff, gpost_ref[...])


def _ffn_call(x, gpre, win, wdown, gpost):
    n, _ = x.shape
    const = lambda shape: pl.BlockSpec(shape, lambda i: (0,) * len(shape))
    return pl.pallas_call(
        _ffn_kernel,
        grid=(n // TM_FFN,),
        in_specs=[
            pl.BlockSpec((TM_FFN, D_MODEL), lambda i: (i, 0)),
            const((1, D_MODEL)),
            const((D_MODEL, 2 * D_FF)),
            const((D_FF, D_MODEL)),
            const((1, D_MODEL)),
        ],
        out_specs=pl.BlockSpec((TM_FFN, D_MODEL), lambda i: (i, 0)),
        out_shape=jax.ShapeDtypeStruct(x.shape, F32),
        compiler_params=pltpu.CompilerParams(
            dimension_semantics=("arbitrary",), vmem_limit_bytes=VMEM_LIMIT),
        name="fox_ffn",
    )(x, gpre, win, wdown, gpost)


def _constants():
    e = np.kron(np.eye(HEADS, dtype=np.float32),
                np.full((HEAD_DIM, HEAD_DIM), 1.0 / HEAD_DIM, np.float32))
    pk = np.zeros((LANES, HEADS * HEAD_PAD), np.float32)
    ok = np.zeros((1, HEADS * HEAD_PAD), np.float32)
    for hd in range(HEADS):
        for j in range(N_SPLIT):
            pk[j * HEADS + hd, hd * HEAD_PAD + AUG_B + j] = -1.0
            ok[0, hd * HEAD_PAD + AUG_A + j] = 1.0
    return jnp.asarray(e, BF16), jnp.asarray(pk, BF16), jnp.asarray(ok)


def _qkv_stage(x, g_pre, wqkv, wfT, b_forget, g_q, g_k):
    e, pk, ok = _constants()
    bf = jnp.broadcast_to(b_forget[:, None], (HEADS, LANES))
    gq = jnp.broadcast_to(jnp.tile(g_q, HEADS)[:, None], (FOX_WIDTH, LANES))
    gk = jnp.tile(g_k, HEADS)[None, :]
    return _qkv_call(x, g_pre[None, :], wqkv, wfT, bf, gq, gk, e, pk, ok)


def kernel(x, g_pre_mix, w_in, b_forget, g_q, g_k, g_sgu, b_sgu, w_spatial, b_spatial,
           w_branch_a, w_branch_b, w_out, g_post_mix, g_pre_ffn, w_ffn_in, w_ffn_down,
           g_post_ffn):
    bsz, s_len, _ = x.shape
    for layer in range(g_pre_mix.shape[0]):
        wqkv, wfT, wug = _win_prep_call(jnp.transpose(w_in[layer]))
        bsp = jnp.repeat(jnp.transpose(b_spatial[layer]), SGU_WIDTH // SGU_GROUPS, axis=1)

        qT, k, vT = _qkv_stage(x, g_pre_mix[layer], wqkv, wfT, b_forget[layer], g_q[layer],
                               g_k[layer])
        aT = _attn_call(qT, k, vT)
        x = _mix_call(x, aT, g_pre_mix[layer][None, :], wug, g_sgu[layer][None, :],
                      b_sgu[layer][None, :], w_spatial[layer], bsp,
                      w_branch_a, w_branch_b, w_out, layer, g_post_mix[layer][None, :])
        x = _ffn_call(x.reshape(bsz * s_len, D_MODEL), g_pre_ffn[layer][None, :],
                      w_ffn_in[layer].astype(BF16), w_ffn_down[layer].astype(BF16),
                      g_post_ffn[layer][None, :]).reshape(bsz, s_len, D_MODEL)
    return x
```

```python
import functools

import numpy as np
import jax
import jax.numpy as jnp
from jax import lax
from jax.experimental import pallas as pl
from jax.experimental.pallas import tpu as pltpu

F32 = jnp.float32
BF16 = jnp.bfloat16

D_MODEL = 1024
CHUNK = 64
HEAD_DIM = 64
FOX_WIDTH = 512
HEADS = 8
SGU_WIDTH = 512
SGU_GROUPS = 8
SGU_WINDOW = 128
D_FF = 2816
EPS = 1e-6

Q_OFF = 0
K_OFF = 512
V_OFF = 1024
F_OFF = 1536
U_OFF = 1544
G_OFF = 2568

LANES = 128
HEAD_PAD = LANES
N_SPLIT = 3
AUG_A = HEAD_DIM
AUG_B = HEAD_DIM + N_SPLIT

TM_QKV = 512
TQ = TM_QKV
TK = TM_QKV
N_Q = 4096 // TQ
HPS = 2
LAG2 = 2
LAG3 = 4
RING = 3
V_ROWS = HEAD_DIM + 16
LOG2E = 1.4426950408889634
TM_MIX = 512
TM_FFN = 512
FF_CHUNK = 1408
NEG = -1e30

VMEM_LIMIT = 56 * 1024 * 1024


def _rms(x, g):
    ms = jnp.mean(x * x, axis=-1, keepdims=True)
    return x * lax.rsqrt(ms + EPS) * g


def _split3(d):
    d1 = d.astype(BF16)
    r1 = d - d1.astype(F32)
    d2 = r1.astype(BF16)
    d3 = (r1 - d2.astype(F32)).astype(BF16)
    return d1, d2, d3


def _win_prep_kernel(wT_ref, wqkv_ref, wfT_ref, wug_ref):
    wqkv_ref[...] = wT_ref[Q_OFF:F_OFF, :].T.astype(BF16)
    wug_ref[...] = wT_ref[U_OFF:, :].T.astype(BF16)
    wfT_ref[...] = wT_ref[F_OFF:F_OFF + 2 * HEADS, :].astype(BF16)


def _win_prep_call(wT):
    cols, d = wT.shape
    chunk = d // 4
    return pl.pallas_call(
        _win_prep_kernel,
        grid=(d // chunk,),
        in_specs=[pl.BlockSpec((cols, chunk), lambda i: (0, i))],
        out_specs=[pl.BlockSpec((chunk, F_OFF), lambda i: (i, 0)),
                   pl.BlockSpec((2 * HEADS, chunk), lambda i: (0, i)),
                   pl.BlockSpec((chunk, cols - U_OFF), lambda i: (i, 0))],
        out_shape=[jax.ShapeDtypeStruct((d, F_OFF), BF16),
                   jax.ShapeDtypeStruct((2 * HEADS, d), BF16),
                   jax.ShapeDtypeStruct((d, cols - U_OFF), BF16)],
        compiler_params=pltpu.CompilerParams(
            dimension_semantics=("arbitrary",), vmem_limit_bytes=VMEM_LIMIT),
        name="fox_win_prep",
    )(wT)


def _qkv_kernel(x_ref, gpre_ref, wqkv_ref, wf_ref, bf_ref, gq_ref, gk_ref, e_ref, pk_ref, ok_ref,
                qT_ref, k_ref, vT_ref, carry_ref):
    @pl.when(pl.program_id(1) == 0)
    def _():
        carry_ref[...] = jnp.zeros_like(carry_ref)

    x = x_ref[0]
    h = _rms(x, gpre_ref[...]).astype(BF16)

    f = lax.dot_general(wf_ref[...], h, (((1,), (1,)), ((), ())),
                        preferred_element_type=F32)[:HEADS, :]
    f = f + jnp.concatenate([bf_ref[...]] * (TM_QKV // LANES), axis=1)
    dT = jnp.minimum(f, 0.0) - jnp.log1p(jnp.exp(-jnp.abs(f)))
    lane = lax.broadcasted_iota(jnp.int32, dT.shape, 1)
    shift = 1
    while shift < TM_QKV:
        dT = dT + jnp.where(lane >= shift, pltpu.roll(dT, shift, 1), 0.0)
        shift *= 2
    dT = dT + jnp.concatenate([carry_ref[...]] * (TM_QKV // LANES), axis=1)
    carry_ref[...] = jnp.broadcast_to(dT[:, TM_QKV - 1:TM_QKV], carry_ref.shape)
    d1, d2, d3 = (p.astype(F32) for p in _split3(dT * LOG2E))

    qT = jnp.dot(h, wqkv_ref[:, Q_OFF:K_OFF], preferred_element_type=F32).T
    gq = jnp.concatenate([gq_ref[...] * (HEAD_DIM ** -0.5 * LOG2E)] * (TM_QKV // LANES), axis=1)
    sub = lax.broadcasted_iota(jnp.int32, (8, TM_QKV), 0)
    pad = jnp.zeros((HEAD_PAD - HEAD_DIM - 8, TM_QKV), F32)
    for hd in range(HEADS):
        rows = qT[hd * HEAD_DIM:(hd + 1) * HEAD_DIM, :]
        ms = jnp.mean(rows * rows, axis=0, keepdims=True)
        qn = rows * lax.rsqrt(ms + EPS) * gq[hd * HEAD_DIM:(hd + 1) * HEAD_DIM, :]
        aug = jnp.where(sub == 0, d1[hd:hd + 1], jnp.where(sub == 1, d2[hd:hd + 1], jnp.where(
            sub == 2, d3[hd:hd + 1], jnp.where(sub < 2 * N_SPLIT, 1.0, 0.0))))
        qT_ref[0, hd, 0] = jnp.concatenate([qn, aug, pad], axis=0).astype(BF16)

    kk = jnp.dot(h, wqkv_ref[:, K_OFF:V_OFF], preferred_element_type=F32)
    ms = jnp.dot((kk * kk).astype(BF16), e_ref[...], preferred_element_type=F32)
    kn = kk * lax.rsqrt(ms + EPS) * gk_ref[...]
    zrows = jnp.zeros((LANES - N_SPLIT * HEADS, TM_QKV), F32)
    dsel = jnp.concatenate([d1, d2, d3, zrows], axis=0).T.astype(BF16)
    aug_k = jnp.dot(dsel, pk_ref[...], preferred_element_type=F32) + ok_ref[...]
    lo = lax.broadcasted_iota(jnp.int32, (TM_QKV, LANES), 1) < HEAD_DIM
    for i in range(HEADS // 2):
        t = kn[:, i * LANES:(i + 1) * LANES]
        for par in range(2):
            hs = slice((2 * i + par) * HEAD_PAD, (2 * i + par + 1) * HEAD_PAD)
            src = t if par == 0 else pltpu.roll(t, HEAD_DIM, 1)
            k_ref[0, :, hs] = jnp.where(lo, src, aug_k[:, hs]).astype(BF16)

    vT = jnp.dot(h, wqkv_ref[:, V_OFF:F_OFF], preferred_element_type=F32).T
    ones_rows = (lax.broadcasted_iota(jnp.int32, (V_ROWS - HEAD_DIM, TM_QKV), 0) == 0).astype(BF16)
    for hd in range(HEADS):
        vT_ref[0, hd, 0, :HEAD_DIM, :] = vT[hd * HEAD_DIM:(hd + 1) * HEAD_DIM, :].astype(BF16)
        vT_ref[0, hd, 0, HEAD_DIM:, :] = ones_rows


def _qkv_call(x, gpre, wqkv, wf, bf, gq, gk, e, pk, ok):
    b, s, _ = x.shape
    n_s = s // TM_QKV
    const = lambda shape: pl.BlockSpec(shape, lambda bi, si: (0,) * len(shape))
    return pl.pallas_call(
        _qkv_kernel,
        grid=(b, n_s),
        in_specs=[
            pl.BlockSpec((1, TM_QKV, D_MODEL), lambda bi, si: (bi, si, 0)),
            const((1, D_MODEL)),
            const((D_MODEL, 3 * FOX_WIDTH)),
            const((2 * HEADS, D_MODEL)),
            const((HEADS, LANES)),
            const((FOX_WIDTH, LANES)),
            const((1, FOX_WIDTH)),
            const((FOX_WIDTH, FOX_WIDTH)),
            const((LANES, HEADS * HEAD_PAD)),
            const((1, HEADS * HEAD_PAD)),
        ],
        out_specs=[
            pl.BlockSpec((1, HEADS, 1, HEAD_PAD, TM_QKV), lambda bi, si: (bi, 0, si, 0, 0)),
            pl.BlockSpec((1, TM_QKV, HEADS * HEAD_PAD), lambda bi, si: (bi, si, 0)),
            pl.BlockSpec((1, HEADS, 1, V_ROWS, TM_QKV), lambda bi, si: (bi, 0, si, 0, 0)),
        ],
        out_shape=[
            jax.ShapeDtypeStruct((b, HEADS, n_s, HEAD_PAD, TM_QKV), BF16),
            jax.ShapeDtypeStruct((b, s, HEADS * HEAD_PAD), BF16),
            jax.ShapeDtypeStruct((b, HEADS, n_s, V_ROWS, TM_QKV), BF16),
        ],
        scratch_shapes=[pltpu.VMEM((HEADS, LANES), F32)],
        compiler_params=pltpu.CompilerParams(
            dimension_semantics=("arbitrary", "arbitrary"), vmem_limit_bytes=VMEM_LIMIT),
        name="fox_qkv",
    )(x, gpre, wqkv, wf, bf, gq, gk, e, pk, ok)


def _attn_kernel(qT_ref, k_ref, vT_ref, o_ref, *scratch):
    sb = scratch[:RING]
    pb = scratch[RING:2 * RING]
    m_ref, acc_ref, bias_ref = scratch[2 * RING:]

    @pl.when((pl.program_id(0) == 0) & (pl.program_id(1) == 0))
    def _():
        row = lax.broadcasted_iota(jnp.int32, (TK, TQ), 0)
        col = lax.broadcasted_iota(jnp.int32, (TK, TQ), 1)
        bias_ref[...] = jnp.where(row <= col, 0.0, NEG)


    def s1(t, r, diag):
        qi, j = t
        start = j * TK if isinstance(j, int) else pl.multiple_of(j * TK, TK)
        out = []
        for hh in range(HPS):
            kj = k_ref[0, pl.ds(start, TK), hh * HEAD_PAD:(hh + 1) * HEAD_PAD]
            s = jnp.dot(kj, qT_ref[0, hh, qi], preferred_element_type=F32)
            if diag:
                s = s + bias_ref[...]
            sb[r][hh] = s
            cmax = jnp.max(s, axis=0, keepdims=True)
            if diag:
                m_before = jnp.full((1, TQ), NEG, F32)
                m_after = cmax
            else:
                m_before = m_ref[hh, qi]
                m_after = jnp.maximum(m_before, cmax)
            m_ref[hh, qi] = m_after
            out.append((m_before, m_after))
        return tuple(out)

    def s2(r, ms):
        out = []
        for hh, (m_before, m_after) in enumerate(ms):
            pb[r][hh] = jnp.exp2(sb[r][hh] - m_after).astype(BF16)
            out.append(jnp.exp2(m_before - m_after))
        return tuple(out)

    def s3(t, r, resc, diag):
        qi, j = t
        for hh, a in enumerate(resc):
            pv = jnp.dot(vT_ref[0, hh, j], pb[r][hh], preferred_element_type=F32)
            if diag:
                acc_ref[hh, qi] = pv
            else:
                acc_ref[hh, qi] = a * acc_ref[hh, qi] + pv

    def pipeline(tiles, diag, successor):
        n_tiles = len(tiles)
        ms = {}
        resc = {}

        def slot(u):
            n3 = u - LAG3
            if 0 <= n3 < n_tiles:
                s3(tiles[n3], n3 % RING, resc.pop(n3), diag)
            n2 = u - LAG2
            if 0 <= n2 < n_tiles:
                resc[n2] = s2(n2 % RING, ms.pop(n2))
            if u < n_tiles:
                ms[u] = s1(tiles[u], u % RING, diag)

        loop_lo = LAG3
        n_iter = (n_tiles - loop_lo) // RING
        loop_hi = loop_lo + n_iter * RING
        for u in range(loop_lo):
            slot(u)

        def body(_, carry):
            hist, ms_q, resc_q = carry
            for i in range(RING):
                hist = hist[1:] + (successor(hist[-1]),)
                u = loop_lo + i
                s3(hist[0], (u - LAG3) % RING, resc_q[0], diag)
                resc_q = resc_q[1:] + (s2((u - LAG2) % RING, ms_q[0]),)
                ms_q = ms_q[1:] + (s1(hist[-1], u % RING, diag),)
            return hist, ms_q, resc_q

        if n_iter > 0:
            first = (tiles[0],) + tuple(tiles[:loop_lo])
            hist0 = tuple((jnp.int32(q), jnp.int32(j)) for q, j in first)
            ms0 = tuple(ms.pop(n) for n in range(loop_lo - LAG2, loop_lo))
            resc0 = tuple(resc.pop(n) for n in range(loop_lo - LAG3, loop_lo - LAG2))
            _, ms1, resc1 = lax.fori_loop(0, n_iter, body, (hist0, ms0, resc0))
            ms.update(zip(range(loop_hi - LAG2, loop_hi), ms1))
            resc.update(zip(range(loop_hi - LAG3, loop_hi - LAG2), resc1))
        for u in range(loop_hi, n_tiles + LAG3):
            slot(u)

    def next_full(t):
        qi, j = t
        wrap = j + 1 == qi
        return jnp.where(wrap, qi + 1, qi), jnp.where(wrap, 0, j + 1)

    pipeline([(q, q) for q in range(N_Q)], True, lambda t: (t[0] + 1, t[1] + 1))
    pipeline([(q, j) for q in range(1, N_Q) for j in range(q)], False, next_full)

    for hh in range(HPS):
        for qi in range(N_Q):
            acc = acc_ref[hh, qi]
            o_ref[0, hh, qi] = (acc[:HEAD_DIM] / acc[HEAD_DIM:HEAD_DIM + 1]).astype(BF16)


def _attn_call(qT, k, vT):
    b = qT.shape[0]
    s = k.shape[1]
    return pl.pallas_call(
        _attn_kernel,
        grid=(b, HEADS // HPS),
        in_specs=[
            pl.BlockSpec((1, HPS, N_Q, HEAD_PAD, TQ), lambda bi, hi: (bi, hi, 0, 0, 0)),
            pl.BlockSpec((1, s, HPS * HEAD_PAD), lambda bi, hi: (bi, 0, hi)),
            pl.BlockSpec((1, HPS, N_Q, V_ROWS, TK), lambda bi, hi: (bi, hi, 0, 0, 0)),
        ],
        out_specs=pl.BlockSpec((1, HPS, N_Q, HEAD_DIM, TQ), lambda bi, hi: (bi, hi, 0, 0, 0)),
        out_shape=jax.ShapeDtypeStruct((b, HEADS, N_Q, HEAD_DIM, TQ), BF16),
        scratch_shapes=(
            [pltpu.VMEM((HPS, TK, TQ), F32)] * RING + [pltpu.VMEM((HPS, TK, TQ), BF16)] * RING
            + [pltpu.VMEM((HPS, N_Q, 1, TQ), F32), pltpu.VMEM((HPS, N_Q, V_ROWS, TQ), F32),
               pltpu.VMEM((TK, TQ), F32)]),
        compiler_params=pltpu.CompilerParams(
            dimension_semantics=("arbitrary", "arbitrary"),
            vmem_limit_bytes=VMEM_LIMIT),
        name="fox_attn",
    )(qT, k, vT)


def _mix_kernel(x_ref, aT_ref, gpre_ref, wug_ref, gsgu_ref, bsgu_ref, ws_ref, bsp_ref,
                wa32_ref, wb32_ref, wout32_ref, gpost_ref, o_ref, mixed_ref,
                wa_ref, wb_ref, wout_ref):
    @pl.when((pl.program_id(0) == 0) & (pl.program_id(1) == 0))
    def _():
        wa_ref[...] = wa32_ref[0].astype(BF16)
        wb_ref[...] = wb32_ref[0].astype(BF16)
        wout_ref[...] = wout32_ref[0].astype(BF16)

    x = x_ref[0]
    h = _rms(x, gpre_ref[...]).astype(BF16)
    uv = jax.nn.gelu(jnp.dot(h, wug_ref[:, :2 * SGU_WIDTH], preferred_element_type=F32))
    u = uv[:, :SGU_WIDTH]
    v = uv[:, SGU_WIDTH:]
    mu = jnp.mean(v, axis=-1, keepdims=True)
    vc = v - mu
    var = jnp.mean(vc * vc, axis=-1, keepdims=True)
    vn = vc * lax.rsqrt(var + EPS) * gsgu_ref[...] + bsgu_ref[...]

    t_idx = lax.broadcasted_iota(jnp.int32, (SGU_WINDOW, SGU_WINDOW), 0)
    s_idx = lax.broadcasted_iota(jnp.int32, (SGU_WINDOW, SGU_WINDOW), 1)
    wmask = (s_idx // CHUNK) <= (t_idx // CHUNK)
    lo = lax.broadcasted_iota(jnp.int32, (SGU_WINDOW, LANES), 1) < (SGU_WIDTH // SGU_GROUPS)
    for gp in range(SGU_GROUPS // 2):
        w_pair = jnp.concatenate(
            [jnp.where(wmask, ws_ref[2 * gp], 0.0), jnp.where(wmask, ws_ref[2 * gp + 1], 0.0)],
            axis=1).astype(BF16)
        for w in range(TM_MIX // SGU_WINDOW):
            vp = vn[w * SGU_WINDOW:(w + 1) * SGU_WINDOW, gp * LANES:(gp + 1) * LANES]
            rhs = jnp.concatenate([jnp.where(lo, vp, 0.0), jnp.where(lo, 0.0, vp)],
                                  axis=0).astype(BF16)
            mixed_ref[w * SGU_WINDOW:(w + 1) * SGU_WINDOW, gp * LANES:(gp + 1) * LANES] = (
                jnp.dot(w_pair, rhs, preferred_element_type=F32)
                + bsp_ref[:, gp * LANES:(gp + 1) * LANES])
    sgu = (u * mixed_ref[...]).astype(BF16)

    y_b = jnp.dot(sgu, wb_ref[...], preferred_element_type=F32)
    aT = aT_ref[0, :, 0].reshape(FOX_WIDTH, TM_MIX)
    y_a = lax.dot_general(aT, wa_ref[...], (((0,), (0,)), ((), ())),
                          preferred_element_type=F32)
    gates = jax.nn.sigmoid(jnp.dot(h, wug_ref[:, 2 * SGU_WIDTH:], preferred_element_type=F32))
    merged = (gates[:, :D_MODEL] * y_a + gates[:, D_MODEL:] * y_b).astype(BF16)
    o = jnp.dot(merged, wout_ref[...], preferred_element_type=F32)
    o_ref[0] = x + _rms(o, gpost_ref[...])


def _mix_call(x, aT, gpre, wug, gsgu, bsgu, ws, bsp, wa, wb, wout, layer, gpost):
    b, s, _ = x.shape
    const = lambda shape: pl.BlockSpec(shape, lambda bi, si: (0,) * len(shape))
    layer_w = lambda shape: pl.BlockSpec((1,) + shape, lambda bi, si: (layer, 0, 0),
                                         pipeline_mode=pl.Buffered(1))
    return pl.pallas_call(
        _mix_kernel,
        grid=(b, s // TM_MIX),
        in_specs=[
            pl.BlockSpec((1, TM_MIX, D_MODEL), lambda bi, si: (bi, si, 0)),
            pl.BlockSpec((1, HEADS, 1, HEAD_DIM, TM_MIX),
                         lambda bi, si: (bi, 0, si // (TQ // TM_MIX), 0, si % (TQ // TM_MIX))),
            const((1, D_MODEL)),
            const((D_MODEL, 2 * SGU_WIDTH + 2 * D_MODEL)),
            const((1, SGU_WIDTH)),
            const((1, SGU_WIDTH)),
            const((SGU_GROUPS, SGU_WINDOW, SGU_WINDOW)),
            const((SGU_WINDOW, SGU_WIDTH)),
            layer_w((FOX_WIDTH, D_MODEL)),
            layer_w((SGU_WIDTH, D_MODEL)),
            layer_w((D_MODEL, D_MODEL)),
            const((1, D_MODEL)),
        ],
        out_specs=pl.BlockSpec((1, TM_MIX, D_MODEL), lambda bi, si: (bi, si, 0)),
        out_shape=jax.ShapeDtypeStruct(x.shape, F32),
        scratch_shapes=[pltpu.VMEM((TM_MIX, SGU_WIDTH), F32),
                        pltpu.VMEM((FOX_WIDTH, D_MODEL), BF16),
                        pltpu.VMEM((SGU_WIDTH, D_MODEL), BF16),
                        pltpu.VMEM((D_MODEL, D_MODEL), BF16)],
        compiler_params=pltpu.CompilerParams(
            dimension_semantics=("arbitrary", "arbitrary"), vmem_limit_bytes=VMEM_LIMIT),
        name="fox_mix",
    )(x, aT, gpre, wug, gsgu, bsgu, ws, bsp, wa, wb, wout, gpost)


def _ffn_kernel(x_ref, gpre_ref, win_ref, wdown_ref, gpost_ref, o_ref):
    x = x_ref[...]
    h = _rms(x, gpre_ref[...]).astype(BF16)
    ff = jnp.zeros((TM_FFN, D_MODEL), F32)
    for c in range(D_FF // FF_CHUNK):
        g = jnp.dot(h, win_ref[:, c * FF_CHUNK:(c + 1) * FF_CHUNK], preferred_element_type=F32)
        u = jnp.dot(h, win_ref[:, D_FF + c * FF_CHUNK:D_FF + (c + 1) * FF_CHUNK],
                    preferred_element_type=F32)
        a = (jax.nn.silu(g) * u).astype(BF16)
        ff = ff + jnp.dot(a, wdown_ref[c * FF_CHUNK:(c + 1) * FF_CHUNK, :],
                          preferred_element_type=F32)
    o_ref[...] = x + _rms(ff, gpost_ref[...])


def _ffn_call(x, gpre, win, wdown, gpost):
    n, _ = x.shape
    const = lambda shape: pl.BlockSpec(shape, lambda i: (0,) * len(shape))
    return pl.pallas_call(
        _ffn_kernel,
        grid=(n // TM_FFN,),
        in_specs=[
            pl.BlockSpec((TM_FFN, D_MODEL), lambda i: (i, 0)),
            const((1, D_MODEL)),
            const((D_MODEL, 2 * D_FF)),
            const((D_FF, D_MODEL)),
            const((1, D_MODEL)),
        ],
        out_specs=pl.BlockSpec((TM_FFN, D_MODEL), lambda i: (i, 0)),
        out_shape=jax.ShapeDtypeStruct(x.shape, F32),
        compiler_params=pltpu.CompilerParams(
            dimension_semantics=("arbitrary",), vmem_limit_bytes=VMEM_LIMIT),
        name="fox_ffn",
    )(x, gpre, win, wdown, gpost)


def _constants():
    e = np.kron(np.eye(HEADS, dtype=np.float32),
                np.full((HEAD_DIM, HEAD_DIM), 1.0 / HEAD_DIM, np.float32))
    pk = np.zeros((LANES, HEADS * HEAD_PAD), np.float32)
    ok = np.zeros((1, HEADS * HEAD_PAD), np.float32)
    for hd in range(HEADS):
        for j in range(N_SPLIT):
            pk[j * HEADS + hd, hd * HEAD_PAD + AUG_B + j] = -1.0
            ok[0, hd * HEAD_PAD + AUG_A + j] = 1.0
    return jnp.asarray(e, BF16), jnp.asarray(pk, BF16), jnp.asarray(ok)


def _qkv_stage(x, g_pre, wqkv, wfT, b_forget, g_q, g_k):
    e, pk, ok = _constants()
    bf = jnp.broadcast_to(b_forget[:, None], (HEADS, LANES))
    gq = jnp.broadcast_to(jnp.tile(g_q, HEADS)[:, None], (FOX_WIDTH, LANES))
    gk = jnp.tile(g_k, HEADS)[None, :]
    return _qkv_call(x, g_pre[None, :], wqkv, wfT, bf, gq, gk, e, pk, ok)


def kernel(x, g_pre_mix, w_in, b_forget, g_q, g_k, g_sgu, b_sgu, w_spatial, b_spatial,
           w_branch_a, w_branch_b, w_out, g_post_mix, g_pre_ffn, w_ffn_in, w_ffn_down,
           g_post_ffn):
    bsz, s_len, _ = x.shape
    for layer in range(g_pre_mix.shape[0]):
        wqkv, wfT, wug = _win_prep_call(jnp.transpose(w_in[layer]))
        bsp = jnp.repeat(jnp.transpose(b_spatial[layer]), SGU_WIDTH // SGU_GROUPS, axis=1)

        qT, k, vT = _qkv_stage(x, g_pre_mix[layer], wqkv, wfT, b_forget[layer], g_q[layer],
                               g_k[layer])
        aT = _attn_call(qT, k, vT)
        x = _mix_call(x, aT, g_pre_mix[layer][None, :], wug, g_sgu[layer][None, :],
                      b_sgu[layer][None, :], w_spatial[layer], bsp,
                      w_branch_a, w_branch_b, w_out, layer, g_post_mix[layer][None, :])
        x = _ffn_call(x.reshape(bsz * s_len, D_MODEL), g_pre_ffn[layer][None, :],
                      w_ffn_in[layer].astype(BF16), w_ffn_down[layer].astype(BF16),
                      g_post_ffn[layer][None, :]).reshape(bsz, s_len, D_MODEL)
    return x
```

```python
import jax
import jax.numpy as jnp
from jax import lax
from jax.experimental import pallas as pl
from jax.experimental.pallas import tpu as pltpu

F32 = jnp.float32
BF16 = jnp.bfloat16

D_MODEL = 1024
CHUNK = 64
HEAD_DIM = 64
FOX_WIDTH = 512
HEADS = 8
SGU_WIDTH = 512
SGU_GROUPS = 8
SGU_WINDOW = 128
D_FF = 2816
EPS = 1e-6

Q_OFF = 0
K_OFF = 512
V_OFF = 1024
F_OFF = 1536
U_OFF = 1544
G_OFF = 2568

LANES = 128
HEAD_PAD = LANES
N_SPLIT = 3

TM_QKV = 512
TQ = TM_QKV
TK = TM_QKV
N_Q = 4096 // TQ
HPS = 2
LAG2 = 2
LAG3 = 4
RING = 3
V_ROWS = HEAD_DIM + 16
LOG2E = 1.4426950408889634
TM_MIX = 512
TM_FFN = 512
MXU_DIM = 256
FF_EDGES = (0, 6 * MXU_DIM, D_FF)
NEG = -1e30

VMEM_LIMIT = 56 * 1024 * 1024


def _rms(x, g):
    ms = jnp.mean(x * x, axis=-1, keepdims=True)
    return x * lax.rsqrt(ms + EPS) * g


def _split3(d):
    d1 = d.astype(BF16)
    r1 = d - d1.astype(F32)
    d2 = r1.astype(BF16)
    d3 = (r1 - d2.astype(F32)).astype(BF16)
    return d1, d2, d3


def _win_prep_kernel(wT_ref, wqkv_ref, wfT_ref, wug_ref):
    wqkv_ref[...] = wT_ref[Q_OFF:F_OFF, :].T.astype(BF16)
    wug_ref[...] = wT_ref[U_OFF:, :].T.astype(BF16)
    wfT_ref[...] = wT_ref[F_OFF:F_OFF + 2 * HEADS, :].astype(BF16)


def _win_prep_call(wT):
    cols, d = wT.shape
    chunk = d // 4
    return pl.pallas_call(
        _win_prep_kernel,
        grid=(d // chunk,),
        in_specs=[pl.BlockSpec((cols, chunk), lambda i: (0, i))],
        out_specs=[pl.BlockSpec((chunk, F_OFF), lambda i: (i, 0)),
                   pl.BlockSpec((2 * HEADS, chunk), lambda i: (0, i)),
                   pl.BlockSpec((chunk, cols - U_OFF), lambda i: (i, 0))],
        out_shape=[jax.ShapeDtypeStruct((d, F_OFF), BF16),
                   jax.ShapeDtypeStruct((2 * HEADS, d), BF16),
                   jax.ShapeDtypeStruct((d, cols - U_OFF), BF16)],
        compiler_params=pltpu.CompilerParams(
            dimension_semantics=("arbitrary",), vmem_limit_bytes=VMEM_LIMIT),
        name="fox_win_prep",
    )(wT)


def _qkv_kernel(x_ref, gpre_ref, wqkv_ref, wf_ref, bf_ref, gq_ref, gk_ref,
                qT_ref, k_ref, vT_ref, carry_ref):
    @pl.when(pl.program_id(1) == 0)
    def _():
        carry_ref[...] = jnp.zeros_like(carry_ref)

    x = x_ref[0]
    h = _rms(x, gpre_ref[...]).astype(BF16)

    f = lax.dot_general(wf_ref[...], h, (((1,), (1,)), ((), ())),
                        preferred_element_type=F32)[:HEADS, :]
    f = f + jnp.concatenate([bf_ref[...]] * (TM_QKV // LANES), axis=1)
    dT = jnp.minimum(f, 0.0) - jnp.log1p(jnp.exp(-jnp.abs(f)))
    lane = lax.broadcasted_iota(jnp.int32, dT.shape, 1)
    shift = 1
    while shift < TM_QKV:
        dT = dT + jnp.where(lane >= shift, pltpu.roll(dT, shift, 1), 0.0)
        shift *= 2
    dT = dT + jnp.concatenate([carry_ref[...]] * (TM_QKV // LANES), axis=1)
    carry_ref[...] = jnp.broadcast_to(dT[:, TM_QKV - 1:TM_QKV], carry_ref.shape)
    d1, d2, d3 = (p.astype(F32) for p in _split3(dT * LOG2E))

    sub = lax.broadcasted_iota(jnp.int32, (8, TM_QKV), 0)
    pad = jnp.zeros((HEAD_PAD - HEAD_DIM - 8, TM_QKV), F32)

    def head_aug_t(cols, g_ref, g_scale, hd, is_q):
        rows = cols[hd * HEAD_DIM:(hd + 1) * HEAD_DIM, :]
        g = jnp.concatenate([g_ref[hd * HEAD_DIM:(hd + 1) * HEAD_DIM, :] * g_scale]
                            * (TM_QKV // LANES), axis=1)
        ms = jnp.mean(rows * rows, axis=0, keepdims=True)
        normed = rows * lax.rsqrt(ms + EPS) * g
        sgn = 1.0 if is_q else -1.0
        pieces = jnp.where(sub % N_SPLIT == 0, d1[hd:hd + 1],
                           jnp.where(sub % N_SPLIT == 1, d2[hd:hd + 1], d3[hd:hd + 1])) * sgn
        d_rows = (sub < N_SPLIT) if is_q else ((sub >= N_SPLIT) & (sub < 2 * N_SPLIT))
        aug = jnp.where(d_rows, pieces, jnp.where(sub < 2 * N_SPLIT, 1.0, 0.0))
        return jnp.concatenate([normed, aug, pad], axis=0)

    qT = jnp.dot(h, wqkv_ref[:, Q_OFF:K_OFF], preferred_element_type=F32).T
    for hd in range(HEADS):
        qT_ref[0, hd, 0] = head_aug_t(qT, gq_ref, HEAD_DIM ** -0.5 * LOG2E, hd, True).astype(BF16)
    kT = jnp.dot(h, wqkv_ref[:, K_OFF:V_OFF], preferred_element_type=F32).T
    for hd in range(HEADS):
        k_ref[0, :, hd * HEAD_PAD:(hd + 1) * HEAD_PAD] = (
            head_aug_t(kT, gk_ref, 1.0, hd, False).T.astype(BF16))

    vT = jnp.dot(h, wqkv_ref[:, V_OFF:F_OFF], preferred_element_type=F32).T
    ones_rows = (lax.broadcasted_iota(jnp.int32, (V_ROWS - HEAD_DIM, TM_QKV), 0) == 0).astype(BF16)
    for hd in range(HEADS):
        vT_ref[0, hd, 0, :HEAD_DIM, :] = vT[hd * HEAD_DIM:(hd + 1) * HEAD_DIM, :].astype(BF16)
        vT_ref[0, hd, 0, HEAD_DIM:, :] = ones_rows


def _qkv_call(x, gpre, wqkv, wf, bf, gq, gk):
    b, s, _ = x.shape
    n_s = s // TM_QKV
    const = lambda shape: pl.BlockSpec(shape, lambda bi, si: (0,) * len(shape))
    return pl.pallas_call(
        _qkv_kernel,
        grid=(b, n_s),
        in_specs=[
            pl.BlockSpec((1, TM_QKV, D_MODEL), lambda bi, si: (bi, si, 0)),
            const((1, D_MODEL)),
            const((D_MODEL, 3 * FOX_WIDTH)),
            const((2 * HEADS, D_MODEL)),
            const((HEADS, LANES)),
            const((FOX_WIDTH, LANES)),
            const((FOX_WIDTH, LANES)),
        ],
        out_specs=[
            pl.BlockSpec((1, HEADS, 1, HEAD_PAD, TM_QKV), lambda bi, si: (bi, 0, si, 0, 0)),
            pl.BlockSpec((1, TM_QKV, HEADS * HEAD_PAD), lambda bi, si: (bi, si, 0)),
            pl.BlockSpec((1, HEADS, 1, V_ROWS, TM_QKV), lambda bi, si: (bi, 0, si, 0, 0)),
        ],
        out_shape=[
            jax.ShapeDtypeStruct((b, HEADS, n_s, HEAD_PAD, TM_QKV), BF16),
            jax.ShapeDtypeStruct((b, s, HEADS * HEAD_PAD), BF16),
            jax.ShapeDtypeStruct((b, HEADS, n_s, V_ROWS, TM_QKV), BF16),
        ],
        scratch_shapes=[pltpu.VMEM((HEADS, LANES), F32)],
        compiler_params=pltpu.CompilerParams(
            dimension_semantics=("arbitrary", "arbitrary"), vmem_limit_bytes=VMEM_LIMIT),
        name="fox_qkv",
    )(x, gpre, wqkv, wf, bf, gq, gk)


def _attn_kernel(qT_ref, k_ref, vT_ref, o_ref, *scratch):
    sb = scratch[:RING]
    pb = scratch[RING:2 * RING]
    m_ref, acc_ref, bias_ref = scratch[2 * RING:]

    @pl.when((pl.program_id(0) == 0) & (pl.program_id(1) == 0))
    def _():
        row = lax.broadcasted_iota(jnp.int32, (TK, TQ), 0)
        col = lax.broadcasted_iota(jnp.int32, (TK, TQ), 1)
        bias_ref[...] = jnp.where(row <= col, 0.0, NEG)


    def s1(t, r, diag):
        qi, j = t
        start = j * TK if isinstance(j, int) else pl.multiple_of(j * TK, TK)
        out = []
        for hh in range(HPS):
            kj = k_ref[0, pl.ds(start, TK), hh * HEAD_PAD:(hh + 1) * HEAD_PAD]
            s = jnp.dot(kj, qT_ref[0, hh, qi], preferred_element_type=F32)
            if diag:
                s = s + bias_ref[...]
            sb[r][hh] = s
            cmax = jnp.max(s, axis=0, keepdims=True)
            if diag:
                m_before = jnp.full((1, TQ), NEG, F32)
                m_after = cmax
            else:
                m_before = m_ref[hh, qi]
                m_after = jnp.maximum(m_before, cmax)
            m_ref[hh, qi] = m_after
            out.append((m_before, m_after))
        return tuple(out)

    def s2(r, ms):
        out = []
        for hh, (m_before, m_after) in enumerate(ms):
            pb[r][hh] = jnp.exp2(sb[r][hh] - m_after).astype(BF16)
            out.append(jnp.exp2(m_before - m_after))
        return tuple(out)

    def s3(t, r, resc, diag):
        qi, j = t
        for hh, a in enumerate(resc):
            pv = jnp.dot(vT_ref[0, hh, j], pb[r][hh], preferred_element_type=F32)
            if diag:
                acc_ref[hh, qi] = pv
            else:
                acc_ref[hh, qi] = a * acc_ref[hh, qi] + pv

    def pipeline(tiles, n_diag, successor):
        n_tiles = len(tiles)
        ms = {}
        resc = {}

        def slot(u):
            n3 = u - LAG3
            if 0 <= n3 < n_tiles:
                s3(tiles[n3], n3 % RING, resc.pop(n3), n3 < n_diag)
            n2 = u - LAG2
            if 0 <= n2 < n_tiles:
                resc[n2] = s2(n2 % RING, ms.pop(n2))
            if u < n_tiles:
                ms[u] = s1(tiles[u], u % RING, u < n_diag)

        loop_lo = n_diag + LAG3
        n_iter = (n_tiles - loop_lo) // RING
        loop_hi = loop_lo + n_iter * RING
        for u in range(loop_lo):
            slot(u)

        def body(_, carry):
            hist, ms_q, resc_q = carry
            for i in range(RING):
                hist = hist[1:] + (successor(hist[-1]),)
                u = loop_lo + i
                s3(hist[0], (u - LAG3) % RING, resc_q[0], False)
                resc_q = resc_q[1:] + (s2((u - LAG2) % RING, ms_q[0]),)
                ms_q = ms_q[1:] + (s1(hist[-1], u % RING, False),)
            return hist, ms_q, resc_q

        if n_iter > 0:
            hist0 = tuple((jnp.int32(q), jnp.int32(j))
                          for q, j in tiles[loop_lo - LAG3 - 1:loop_lo])
            ms0 = tuple(ms.pop(n) for n in range(loop_lo - LAG2, loop_lo))
            resc0 = tuple(resc.pop(n) for n in range(loop_lo - LAG3, loop_lo - LAG2))
            _, ms1, resc1 = lax.fori_loop(0, n_iter, body, (hist0, ms0, resc0))
            ms.update(zip(range(loop_hi - LAG2, loop_hi), ms1))
            resc.update(zip(range(loop_hi - LAG3, loop_hi - LAG2), resc1))
        for u in range(loop_hi, n_tiles + LAG3):
            slot(u)

    def next_full(t):
        qi, j = t
        wrap = j + 1 == qi
        return jnp.where(wrap, qi + 1, qi), jnp.where(wrap, 0, j + 1)

    pipeline([(q, q) for q in range(N_Q)] + [(q, j) for q in range(1, N_Q) for j in range(q)],
             N_Q, next_full)

    for hh in range(HPS):
        for qi in range(N_Q):
            acc = acc_ref[hh, qi]
            o_ref[0, hh, qi] = (acc[:HEAD_DIM] / acc[HEAD_DIM:HEAD_DIM + 1]).astype(BF16)


def _attn_call(qT, k, vT):
    b = qT.shape[0]
    s = k.shape[1]
    return pl.pallas_call(
        _attn_kernel,
        grid=(b, HEADS // HPS),
        in_specs=[
            pl.BlockSpec((1, HPS, N_Q, HEAD_PAD, TQ), lambda bi, hi: (bi, hi, 0, 0, 0)),
            pl.BlockSpec((1, s, HPS * HEAD_PAD), lambda bi, hi: (bi, 0, hi)),
            pl.BlockSpec((1, HPS, N_Q, V_ROWS, TK), lambda bi, hi: (bi, hi, 0, 0, 0)),
        ],
        out_specs=pl.BlockSpec((1, HPS, N_Q, HEAD_DIM, TQ), lambda bi, hi: (bi, hi, 0, 0, 0)),
        out_shape=jax.ShapeDtypeStruct((b, HEADS, N_Q, HEAD_DIM, TQ), BF16),
        scratch_shapes=(
            [pltpu.VMEM((HPS, TK, TQ), F32)] * RING + [pltpu.VMEM((HPS, TK, TQ), BF16)] * RING
            + [pltpu.VMEM((HPS, N_Q, 1, TQ), F32), pltpu.VMEM((HPS, N_Q, V_ROWS, TQ), F32),
               pltpu.VMEM((TK, TQ), F32)]),
        compiler_params=pltpu.CompilerParams(
            dimension_semantics=("arbitrary", "arbitrary"),
            vmem_limit_bytes=VMEM_LIMIT),
        name="fox_attn",
    )(qT, k, vT)


def _mix_kernel(x_ref, aT_ref, gpre_ref, wug_ref, gsgu_ref, bsgu_ref, ws_ref, bsp_ref,
                wa32_ref, wb32_ref, wout32_ref, gpost_ref, o_ref, mixed_ref,
                wa_ref, wb_ref, wout_ref):
    @pl.when((pl.program_id(0) == 0) & (pl.program_id(1) == 0))
    def _():
        wa_ref[...] = wa32_ref[0].astype(BF16)
        wb_ref[...] = wb32_ref[0].astype(BF16)
        wout_ref[...] = wout32_ref[0].astype(BF16)

    x = x_ref[0]
    h = _rms(x, gpre_ref[...]).astype(BF16)
    uv = jax.nn.gelu(jnp.dot(h, wug_ref[:, :2 * SGU_WIDTH], preferred_element_type=F32))
    u = uv[:, :SGU_WIDTH]
    v = uv[:, SGU_WIDTH:]
    mu = jnp.mean(v, axis=-1, keepdims=True)
    vc = v - mu
    var = jnp.mean(vc * vc, axis=-1, keepdims=True)
    vn = vc * lax.rsqrt(var + EPS) * gsgu_ref[...] + bsgu_ref[...]

    t_idx = lax.broadcasted_iota(jnp.int32, (SGU_WINDOW, SGU_WINDOW), 0)
    s_idx = lax.broadcasted_iota(jnp.int32, (SGU_WINDOW, SGU_WINDOW), 1)
    wmask = (s_idx // CHUNK) <= (t_idx // CHUNK)
    lo = lax.broadcasted_iota(jnp.int32, (SGU_WINDOW, LANES), 1) < (SGU_WIDTH // SGU_GROUPS)
    for gp in range(SGU_GROUPS // 2):
        w_pair = jnp.concatenate(
            [jnp.where(wmask, ws_ref[2 * gp], 0.0), jnp.where(wmask, ws_ref[2 * gp + 1], 0.0)],
            axis=1).astype(BF16)
        for w in range(TM_MIX // SGU_WINDOW):
            vp = vn[w * SGU_WINDOW:(w + 1) * SGU_WINDOW, gp * LANES:(gp + 1) * LANES]
            rhs = jnp.concatenate([jnp.where(lo, vp, 0.0), jnp.where(lo, 0.0, vp)],
                                  axis=0).astype(BF16)
            mixed_ref[w * SGU_WINDOW:(w + 1) * SGU_WINDOW, gp * LANES:(gp + 1) * LANES] = (
                jnp.dot(w_pair, rhs, preferred_element_type=F32)
                + bsp_ref[:, gp * LANES:(gp + 1) * LANES])
    sgu = (u * mixed_ref[...]).astype(BF16)

    y_b = jnp.dot(sgu, wb_ref[...], preferred_element_type=F32)
    aT = aT_ref[0, :, 0].reshape(FOX_WIDTH, TM_MIX)
    y_a = lax.dot_general(aT, wa_ref[...], (((0,), (0,)), ((), ())),
                          preferred_element_type=F32)
    gates = jax.nn.sigmoid(jnp.dot(h, wug_ref[:, 2 * SGU_WIDTH:], preferred_element_type=F32))
    merged = (gates[:, :D_MODEL] * y_a + gates[:, D_MODEL:] * y_b).astype(BF16)
    o = jnp.dot(merged, wout_ref[...], preferred_element_type=F32)
    o_ref[0] = x + _rms(o, gpost_ref[...])


def _mix_call(x, aT, gpre, wug, gsgu, bsgu, ws, bsp, wa, wb, wout, layer, gpost):
    b, s, _ = x.shape
    const = lambda shape: pl.BlockSpec(shape, lambda bi, si: (0,) * len(shape))
    layer_w = lambda shape: pl.BlockSpec((1,) + shape, lambda bi, si: (layer, 0, 0),
                                         pipeline_mode=pl.Buffered(1))
    return pl.pallas_call(
        _mix_kernel,
        grid=(b, s // TM_MIX),
        in_specs=[
            pl.BlockSpec((1, TM_MIX, D_MODEL), lambda bi, si: (bi, si, 0)),
            pl.BlockSpec((1, HEADS, 1, HEAD_DIM, TM_MIX),
                         lambda bi, si: (bi, 0, si // (TQ // TM_MIX), 0, si % (TQ // TM_MIX))),
            const((1, D_MODEL)),
            const((D_MODEL, 2 * SGU_WIDTH + 2 * D_MODEL)),
            const((1, SGU_WIDTH)),
            const((1, SGU_WIDTH)),
            const((SGU_GROUPS, SGU_WINDOW, SGU_WINDOW)),
            const((SGU_WINDOW, SGU_WIDTH)),
            layer_w((FOX_WIDTH, D_MODEL)),
            layer_w((SGU_WIDTH, D_MODEL)),
            layer_w((D_MODEL, D_MODEL)),
            const((1, D_MODEL)),
        ],
        out_specs=pl.BlockSpec((1, TM_MIX, D_MODEL), lambda bi, si: (bi, si, 0)),
        out_shape=jax.ShapeDtypeStruct(x.shape, F32),
        scratch_shapes=[pltpu.VMEM((TM_MIX, SGU_WIDTH), F32),
                        pltpu.VMEM((FOX_WIDTH, D_MODEL), BF16),
                        pltpu.VMEM((SGU_WIDTH, D_MODEL), BF16),
                        pltpu.VMEM((D_MODEL, D_MODEL), BF16)],
        compiler_params=pltpu.CompilerParams(
            dimension_semantics=("arbitrary", "arbitrary"), vmem_limit_bytes=VMEM_LIMIT),
        name="fox_mix",
    )(x, aT, gpre, wug, gsgu, bsgu, ws, bsp, wa, wb, wout, gpost)


def _ffn_kernel(x_ref, gpre_ref, win_ref, wdown_ref, gpost_ref, o_ref):
    x = x_ref[...]
    h = _rms(x, gpre_ref[...]).astype(BF16)
    ff = jnp.zeros((TM_FFN, D_MODEL), F32)
    for lo, hi in zip(FF_EDGES[:-1], FF_EDGES[1:]):
        g = jnp.dot(h, win_ref[:, lo:hi], preferred_element_type=F32)
        u = jnp.dot(h, win_ref[:, D_FF + lo:D_FF + hi], preferred_element_type=F32)
        a = (jax.nn.silu(g) * u).astype(BF16)
        ff = ff + jnp.dot(a, wdown_ref[lo:hi, :], preferred_element_type=F32)
    o_ref[...] = x + _rms(ff, gpost_ref[...])


def _ffn_call(x, gpre, win, wdown, gpost):
    n, _ = x.shape
    const = lambda shape: pl.BlockSpec(shape, lambda i: (0,) * len(shape))
    return pl.pallas_call(
        _ffn_kernel,
        grid=(n // TM_FFN,),
        in_specs=[
            pl.BlockSpec((TM_FFN, D_MODEL), lambda i: (i, 0)),
            const((1, D_MODEL)),
            const((D_MODEL, 2 * D_FF)),
            const((D_FF, D_MODEL)),
            const((1, D_MODEL)),
        ],
        out_specs=pl.BlockSpec((TM_FFN, D_MODEL), lambda i: (i, 0)),
        out_shape=jax.ShapeDtypeStruct(x.shape, F32),
        compiler_params=pltpu.CompilerParams(
            dimension_semantics=("arbitrary",), vmem_limit_bytes=VMEM_LIMIT),
        name="fox_ffn",
    )(x, gpre, win, wdown, gpost)


def _qkv_stage(x, g_pre, wqkv, wfT, b_forget, g_q, g_k):
    bf = jnp.broadcast_to(b_forget[:, None], (HEADS, LANES))
    gq = jnp.broadcast_to(jnp.tile(g_q, HEADS)[:, None], (FOX_WIDTH, LANES))
    gk = jnp.broadcast_to(jnp.tile(g_k, HEADS)[:, None], (FOX_WIDTH, LANES))
    return _qkv_call(x, g_pre[None, :], wqkv, wfT, bf, gq, gk)


def kernel(x, g_pre_mix, w_in, b_forget, g_q, g_k, g_sgu, b_sgu, w_spatial, b_spatial,
           w_branch_a, w_branch_b, w_out, g_post_mix, g_pre_ffn, w_ffn_in, w_ffn_down,
           g_post_ffn):
    bsz, s_len, _ = x.shape
    for layer in range(g_pre_mix.shape[0]):
        wqkv, wfT, wug = _win_prep_call(jnp.transpose(w_in[layer]))
        bsp = jnp.repeat(jnp.transpose(b_spatial[layer]), SGU_WIDTH // SGU_GROUPS, axis=1)

        qT, k, vT = _qkv_stage(x, g_pre_mix[layer], wqkv, wfT, b_forget[layer], g_q[layer],
                               g_k[layer])
        aT = _attn_call(qT, k, vT)
        x = _mix_call(x, aT, g_pre_mix[layer][None, :], wug, g_sgu[layer][None, :],
                      b_sgu[layer][None, :], w_spatial[layer], bsp,
                      w_branch_a, w_branch_b, w_out, layer, g_post_mix[layer][None, :])
        x = _ffn_call(x.reshape(bsz * s_len, D_MODEL), g_pre_ffn[layer][None, :],
                      w_ffn_in[layer].astype(BF16), w_ffn_down[layer].astype(BF16),
                      g_post_ffn[layer][None, :]).reshape(bsz, s_len, D_MODEL)
    return x
```

```python
import jax
import jax.numpy as jnp
from jax import lax
from jax.experimental import pallas as pl
from jax.experimental.pallas import tpu as pltpu

F32 = jnp.float32
BF16 = jnp.bfloat16

D_MODEL = 1024
CHUNK = 64
HEAD_DIM = 64
FOX_WIDTH = 512
HEADS = 8
SGU_WIDTH = 512
SGU_GROUPS = 8
SGU_WINDOW = 128
D_FF = 2816
EPS = 1e-6

Q_OFF = 0
K_OFF = 512
V_OFF = 1024
F_OFF = 1536
U_OFF = 1544
G_OFF = 2568

LANES = 128
HEAD_PAD = LANES
N_SPLIT = 3

TM_QKV = 512
TQ = TM_QKV
TK = TM_QKV
N_Q = 4096 // TQ
HPS = 2
LAG2 = 2
LAG3 = 4
RING = 3
V_ROWS = HEAD_DIM + 16
LOG2E = 1.4426950408889634
TM_MIX = 512
TM_FFN = 512
MXU_DIM = 256
FF_EDGES = (0, 6 * MXU_DIM, D_FF)
NEG = -1e30

VMEM_LIMIT = 56 * 1024 * 1024


def _rms(x, g):
    ms = jnp.mean(x * x, axis=-1, keepdims=True)
    return x * lax.rsqrt(ms + EPS) * g


def _split3(d):
    d1 = d.astype(BF16)
    r1 = d - d1.astype(F32)
    d2 = r1.astype(BF16)
    d3 = (r1 - d2.astype(F32)).astype(BF16)
    return d1, d2, d3


def _win_prep_kernel(wT_ref, wqkv_ref, wfT_ref, wug_ref):
    wqkv_ref[...] = wT_ref[Q_OFF:F_OFF, :].T.astype(BF16)
    wug_ref[...] = wT_ref[U_OFF:, :].T.astype(BF16)
    wfT_ref[...] = wT_ref[F_OFF:F_OFF + 2 * HEADS, :].astype(BF16)


def _win_prep_call(wT):
    cols, d = wT.shape
    chunk = d // 4
    return pl.pallas_call(
        _win_prep_kernel,
        grid=(d // chunk,),
        in_specs=[pl.BlockSpec((cols, chunk), lambda i: (0, i))],
        out_specs=[pl.BlockSpec((chunk, F_OFF), lambda i: (i, 0)),
                   pl.BlockSpec((2 * HEADS, chunk), lambda i: (0, i)),
                   pl.BlockSpec((chunk, cols - U_OFF), lambda i: (i, 0))],
        out_shape=[jax.ShapeDtypeStruct((d, F_OFF), BF16),
                   jax.ShapeDtypeStruct((2 * HEADS, d), BF16),
                   jax.ShapeDtypeStruct((d, cols - U_OFF), BF16)],
        compiler_params=pltpu.CompilerParams(
            dimension_semantics=("arbitrary",), vmem_limit_bytes=VMEM_LIMIT),
        name="fox_win_prep",
    )(wT)


def _qkv_kernel(x_ref, gpre_ref, wqkv_ref, wf_ref, bf_ref, gq_ref, gk_ref,
                qT_ref, k_ref, vT_ref, carry_ref):
    @pl.when(pl.program_id(1) == 0)
    def _():
        carry_ref[...] = jnp.zeros_like(carry_ref)

    x = x_ref[0]
    h = _rms(x, gpre_ref[...]).astype(BF16)

    f = lax.dot_general(wf_ref[...], h, (((1,), (1,)), ((), ())),
                        preferred_element_type=F32)[:HEADS, :]
    f = f + jnp.concatenate([bf_ref[...]] * (TM_QKV // LANES), axis=1)
    dT = jnp.minimum(f, 0.0) - jnp.log1p(jnp.exp(-jnp.abs(f)))
    lane = lax.broadcasted_iota(jnp.int32, dT.shape, 1)
    shift = 1
    while shift < TM_QKV:
        dT = dT + jnp.where(lane >= shift, pltpu.roll(dT, shift, 1), 0.0)
        shift *= 2
    dT = dT + jnp.concatenate([carry_ref[...]] * (TM_QKV // LANES), axis=1)
    carry_ref[...] = jnp.broadcast_to(dT[:, TM_QKV - 1:TM_QKV], carry_ref.shape)
    d1, d2, d3 = (p.astype(F32) for p in _split3(dT * LOG2E))

    sub = lax.broadcasted_iota(jnp.int32, (8, TM_QKV), 0)
    pad = jnp.zeros((HEAD_PAD - HEAD_DIM - 8, TM_QKV), F32)

    def head_aug_t(cols, g_ref, g_scale, hd, is_q):
        rows = cols[hd * HEAD_DIM:(hd + 1) * HEAD_DIM, :]
        g = jnp.concatenate([g_ref[hd * HEAD_DIM:(hd + 1) * HEAD_DIM, :] * g_scale]
                            * (TM_QKV // LANES), axis=1)
        ms = jnp.mean(rows * rows, axis=0, keepdims=True)
        normed = rows * lax.rsqrt(ms + EPS) * g
        sgn = 1.0 if is_q else -1.0
        pieces = jnp.where(sub % N_SPLIT == 0, d1[hd:hd + 1],
                           jnp.where(sub % N_SPLIT == 1, d2[hd:hd + 1], d3[hd:hd + 1])) * sgn
        d_rows = (sub < N_SPLIT) if is_q else ((sub >= N_SPLIT) & (sub < 2 * N_SPLIT))
        aug = jnp.where(d_rows, pieces, jnp.where(sub < 2 * N_SPLIT, 1.0, 0.0))
        return jnp.concatenate([normed, aug, pad], axis=0)

    qT = jnp.dot(h, wqkv_ref[:, Q_OFF:K_OFF], preferred_element_type=F32).T
    for hd in range(HEADS):
        qT_ref[0, hd, 0] = head_aug_t(qT, gq_ref, HEAD_DIM ** -0.5 * LOG2E, hd, True).astype(BF16)
    kT = jnp.dot(h, wqkv_ref[:, K_OFF:V_OFF], preferred_element_type=F32).T
    for hd in range(HEADS):
        k_ref[0, :, hd * HEAD_PAD:(hd + 1) * HEAD_PAD] = (
            head_aug_t(kT, gk_ref, 1.0, hd, False).T.astype(BF16))

    vT = jnp.dot(h, wqkv_ref[:, V_OFF:F_OFF], preferred_element_type=F32).T
    ones_rows = (lax.broadcasted_iota(jnp.int32, (V_ROWS - HEAD_DIM, TM_QKV), 0) == 0).astype(BF16)
    for hd in range(HEADS):
        vT_ref[0, hd, 0, :HEAD_DIM, :] = vT[hd * HEAD_DIM:(hd + 1) * HEAD_DIM, :].astype(BF16)
        vT_ref[0, hd, 0, HEAD_DIM:, :] = ones_rows


def _qkv_call(x, gpre, wqkv, wf, bf, gq, gk):
    b, s, _ = x.shape
    n_s = s // TM_QKV
    const = lambda shape: pl.BlockSpec(shape, lambda bi, si: (0,) * len(shape))
    return pl.pallas_call(
        _qkv_kernel,
        grid=(b, n_s),
        in_specs=[
            pl.BlockSpec((1, TM_QKV, D_MODEL), lambda bi, si: (bi, si, 0)),
            const((1, D_MODEL)),
            const((D_MODEL, 3 * FOX_WIDTH)),
            const((2 * HEADS, D_MODEL)),
            const((HEADS, LANES)),
            const((FOX_WIDTH, LANES)),
            const((FOX_WIDTH, LANES)),
        ],
        out_specs=[
            pl.BlockSpec((1, HEADS, 1, HEAD_PAD, TM_QKV), lambda bi, si: (bi, 0, si, 0, 0)),
            pl.BlockSpec((1, TM_QKV, HEADS * HEAD_PAD), lambda bi, si: (bi, si, 0)),
            pl.BlockSpec((1, HEADS, 1, V_ROWS, TM_QKV), lambda bi, si: (bi, 0, si, 0, 0)),
        ],
        out_shape=[
            jax.ShapeDtypeStruct((b, HEADS, n_s, HEAD_PAD, TM_QKV), BF16),
            jax.ShapeDtypeStruct((b, s, HEADS * HEAD_PAD), BF16),
            jax.ShapeDtypeStruct((b, HEADS, n_s, V_ROWS, TM_QKV), BF16),
        ],
        scratch_shapes=[pltpu.VMEM((HEADS, LANES), F32)],
        compiler_params=pltpu.CompilerParams(
            dimension_semantics=("arbitrary", "arbitrary"), vmem_limit_bytes=VMEM_LIMIT),
        name="fox_qkv",
    )(x, gpre, wqkv, wf, bf, gq, gk)


def _attn_kernel(qT_ref, k_ref, vT_ref, wfin32_ref, wfdn32_ref, o_ref, wfin_ref, wfdn_ref,
                 *scratch):
    wfin_ref[...] = wfin32_ref[0].astype(BF16)
    wfdn_ref[...] = wfdn32_ref[0].astype(BF16)

    sb = scratch[:RING]
    pb = scratch[RING:2 * RING]
    m_ref, acc_ref, bias_ref = scratch[2 * RING:]

    @pl.when((pl.program_id(0) == 0) & (pl.program_id(1) == 0))
    def _():
        row = lax.broadcasted_iota(jnp.int32, (TK, TQ), 0)
        col = lax.broadcasted_iota(jnp.int32, (TK, TQ), 1)
        bias_ref[...] = jnp.where(row <= col, 0.0, NEG)


    def s1(t, r, diag):
        qi, j = t
        start = j * TK if isinstance(j, int) else pl.multiple_of(j * TK, TK)
        out = []
        for hh in range(HPS):
            kj = k_ref[0, pl.ds(start, TK), hh * HEAD_PAD:(hh + 1) * HEAD_PAD]
            s = jnp.dot(kj, qT_ref[0, hh, qi], preferred_element_type=F32)
            if diag:
                s = s + bias_ref[...]
            sb[r][hh] = s
            cmax = jnp.max(s, axis=0, keepdims=True)
            if diag:
                m_before = jnp.full((1, TQ), NEG, F32)
                m_after = cmax
            else:
                m_before = m_ref[hh, qi]
                m_after = jnp.maximum(m_before, cmax)
            m_ref[hh, qi] = m_after
            out.append((m_before, m_after))
        return tuple(out)

    def s2(r, ms):
        out = []
        for hh, (m_before, m_after) in enumerate(ms):
            pb[r][hh] = jnp.exp2(sb[r][hh] - m_after).astype(BF16)
            out.append(jnp.exp2(m_before - m_after))
        return tuple(out)

    def s3(t, r, resc, diag):
        qi, j = t
        for hh, a in enumerate(resc):
            pv = jnp.dot(vT_ref[0, hh, j], pb[r][hh], preferred_element_type=F32)
            if diag:
                acc_ref[hh, qi] = pv
            else:
                acc_ref[hh, qi] = a * acc_ref[hh, qi] + pv

    def pipeline(tiles, n_diag, successor):
        n_tiles = len(tiles)
        ms = {}
        resc = {}

        def slot(u):
            n3 = u - LAG3
            if 0 <= n3 < n_tiles:
                s3(tiles[n3], n3 % RING, resc.pop(n3), n3 < n_diag)
            n2 = u - LAG2
            if 0 <= n2 < n_tiles:
                resc[n2] = s2(n2 % RING, ms.pop(n2))
            if u < n_tiles:
                ms[u] = s1(tiles[u], u % RING, u < n_diag)

        loop_lo = n_diag + LAG3
        n_iter = (n_tiles - loop_lo) // RING
        loop_hi = loop_lo + n_iter * RING
        for u in range(loop_lo):
            slot(u)

        def body(_, carry):
            hist, ms_q, resc_q = carry
            for i in range(RING):
                hist = hist[1:] + (successor(hist[-1]),)
                u = loop_lo + i
                s3(hist[0], (u - LAG3) % RING, resc_q[0], False)
                resc_q = resc_q[1:] + (s2((u - LAG2) % RING, ms_q[0]),)
                ms_q = ms_q[1:] + (s1(hist[-1], u % RING, False),)
            return hist, ms_q, resc_q

        if n_iter > 0:
            hist0 = tuple((jnp.int32(q), jnp.int32(j))
                          for q, j in tiles[loop_lo - LAG3 - 1:loop_lo])
            ms0 = tuple(ms.pop(n) for n in range(loop_lo - LAG2, loop_lo))
            resc0 = tuple(resc.pop(n) for n in range(loop_lo - LAG3, loop_lo - LAG2))
            _, ms1, resc1 = lax.fori_loop(0, n_iter, body, (hist0, ms0, resc0))
            ms.update(zip(range(loop_hi - LAG2, loop_hi), ms1))
            resc.update(zip(range(loop_hi - LAG3, loop_hi - LAG2), resc1))
        for u in range(loop_hi, n_tiles + LAG3):
            slot(u)

    def next_full(t):
        qi, j = t
        wrap = j + 1 == qi
        return jnp.where(wrap, qi + 1, qi), jnp.where(wrap, 0, j + 1)

    pipeline([(q, q) for q in range(N_Q)] + [(q, j) for q in range(1, N_Q) for j in range(q)],
             N_Q, next_full)

    for hh in range(HPS):
        for qi in range(N_Q):
            acc = acc_ref[hh, qi]
            o_ref[0, hh, qi] = (acc[:HEAD_DIM] / acc[HEAD_DIM:HEAD_DIM + 1]).astype(BF16)


def _attn_call(qT, k, vT, w_ffn_in, w_ffn_down, layer):
    b = qT.shape[0]
    s = k.shape[1]
    n_hg = HEADS // HPS
    steps = b * n_hg
    _, d, ff2 = w_ffn_in.shape
    _, ff, _ = w_ffn_down.shape
    slab = lambda bi, hi: (bi * n_hg + hi, 0)
    return pl.pallas_call(
        _attn_kernel,
        grid=(b, n_hg),
        in_specs=[
            pl.BlockSpec((1, HPS, N_Q, HEAD_PAD, TQ), lambda bi, hi: (bi, hi, 0, 0, 0)),
            pl.BlockSpec((1, s, HPS * HEAD_PAD), lambda bi, hi: (bi, 0, hi)),
            pl.BlockSpec((1, HPS, N_Q, V_ROWS, TK), lambda bi, hi: (bi, hi, 0, 0, 0)),
            pl.BlockSpec((1, d // steps, ff2), lambda bi, hi: (layer,) + slab(bi, hi)),
            pl.BlockSpec((1, ff // steps, d), lambda bi, hi: (layer,) + slab(bi, hi)),
        ],
        out_specs=[
            pl.BlockSpec((1, HPS, N_Q, HEAD_DIM, TQ), lambda bi, hi: (bi, hi, 0, 0, 0)),
            pl.BlockSpec((d // steps, ff2), slab),
            pl.BlockSpec((ff // steps, d), slab),
        ],
        out_shape=[
            jax.ShapeDtypeStruct((b, HEADS, N_Q, HEAD_DIM, TQ), BF16),
            jax.ShapeDtypeStruct((d, ff2), BF16),
            jax.ShapeDtypeStruct((ff, d), BF16),
        ],
        scratch_shapes=(
            [pltpu.VMEM((HPS, TK, TQ), F32)] * RING + [pltpu.VMEM((HPS, TK, TQ), BF16)] * RING
            + [pltpu.VMEM((HPS, N_Q, 1, TQ), F32), pltpu.VMEM((HPS, N_Q, V_ROWS, TQ), F32),
               pltpu.VMEM((TK, TQ), F32)]),
        compiler_params=pltpu.CompilerParams(
            dimension_semantics=("arbitrary", "arbitrary"),
            vmem_limit_bytes=VMEM_LIMIT),
        name="fox_attn",
    )(qT, k, vT, w_ffn_in, w_ffn_down)


def _mix_kernel(x_ref, aT_ref, gpre_ref, wug_ref, gsgu_ref, bsgu_ref, ws_ref, bsp_ref,
                wa32_ref, wb32_ref, wout32_ref, gpost_ref, o_ref, mixed_ref,
                wa_ref, wb_ref, wout_ref):
    @pl.when((pl.program_id(0) == 0) & (pl.program_id(1) == 0))
    def _():
        wa_ref[...] = wa32_ref[0].astype(BF16)
        wb_ref[...] = wb32_ref[0].astype(BF16)
        wout_ref[...] = wout32_ref[0].astype(BF16)

    x = x_ref[0]
    h = _rms(x, gpre_ref[...]).astype(BF16)
    uv = jax.nn.gelu(jnp.dot(h, wug_ref[:, :2 * SGU_WIDTH], preferred_element_type=F32))
    u = uv[:, :SGU_WIDTH]
    v = uv[:, SGU_WIDTH:]
    mu = jnp.mean(v, axis=-1, keepdims=True)
    vc = v - mu
    var = jnp.mean(vc * vc, axis=-1, keepdims=True)
    vn = vc * lax.rsqrt(var + EPS) * gsgu_ref[...] + bsgu_ref[...]

    t_idx = lax.broadcasted_iota(jnp.int32, (SGU_WINDOW, SGU_WINDOW), 0)
    s_idx = lax.broadcasted_iota(jnp.int32, (SGU_WINDOW, SGU_WINDOW), 1)
    wmask = (s_idx // CHUNK) <= (t_idx // CHUNK)
    lo = lax.broadcasted_iota(jnp.int32, (SGU_WINDOW, LANES), 1) < (SGU_WIDTH // SGU_GROUPS)
    for gp in range(SGU_GROUPS // 2):
        w_pair = jnp.concatenate(
            [jnp.where(wmask, ws_ref[2 * gp], 0.0), jnp.where(wmask, ws_ref[2 * gp + 1], 0.0)],
            axis=1).astype(BF16)
        for w in range(TM_MIX // SGU_WINDOW):
            vp = vn[w * SGU_WINDOW:(w + 1) * SGU_WINDOW, gp * LANES:(gp + 1) * LANES]
            rhs = jnp.concatenate([jnp.where(lo, vp, 0.0), jnp.where(lo, 0.0, vp)],
                                  axis=0).astype(BF16)
            mixed_ref[w * SGU_WINDOW:(w + 1) * SGU_WINDOW, gp * LANES:(gp + 1) * LANES] = (
                jnp.dot(w_pair, rhs, preferred_element_type=F32)
                + bsp_ref[:, gp * LANES:(gp + 1) * LANES])
    sgu = (u * mixed_ref[...]).astype(BF16)

    y_b = jnp.dot(sgu, wb_ref[...], preferred_element_type=F32)
    aT = aT_ref[0, :, 0].reshape(FOX_WIDTH, TM_MIX)
    y_a = lax.dot_general(aT, wa_ref[...], (((0,), (0,)), ((), ())),
                          preferred_element_type=F32)
    gates = jax.nn.sigmoid(jnp.dot(h, wug_ref[:, 2 * SGU_WIDTH:], preferred_element_type=F32))
    merged = (gates[:, :D_MODEL] * y_a + gates[:, D_MODEL:] * y_b).astype(BF16)
    o = jnp.dot(merged, wout_ref[...], preferred_element_type=F32)
    o_ref[0] = x + _rms(o, gpost_ref[...])


def _mix_call(x, aT, gpre, wug, gsgu, bsgu, ws, bsp, wa, wb, wout, layer, gpost):
    b, s, _ = x.shape
    const = lambda shape: pl.BlockSpec(shape, lambda bi, si: (0,) * len(shape))
    layer_w = lambda shape: pl.BlockSpec((1,) + shape, lambda bi, si: (layer, 0, 0),
                                         pipeline_mode=pl.Buffered(1))
    return pl.pallas_call(
        _mix_kernel,
        grid=(b, s // TM_MIX),
        in_specs=[
            pl.BlockSpec((1, TM_MIX, D_MODEL), lambda bi, si: (bi, si, 0)),
            pl.BlockSpec((1, HEADS, 1, HEAD_DIM, TM_MIX),
                         lambda bi, si: (bi, 0, si // (TQ // TM_MIX), 0, si % (TQ // TM_MIX))),
            const((1, D_MODEL)),
            const((D_MODEL, 2 * SGU_WIDTH + 2 * D_MODEL)),
            const((1, SGU_WIDTH)),
            const((1, SGU_WIDTH)),
            const((SGU_GROUPS, SGU_WINDOW, SGU_WINDOW)),
            const((SGU_WINDOW, SGU_WIDTH)),
            layer_w((FOX_WIDTH, D_MODEL)),
            layer_w((SGU_WIDTH, D_MODEL)),
            layer_w((D_MODEL, D_MODEL)),
            const((1, D_MODEL)),
        ],
        out_specs=pl.BlockSpec((1, TM_MIX, D_MODEL), lambda bi, si: (bi, si, 0)),
        out_shape=jax.ShapeDtypeStruct(x.shape, F32),
        scratch_shapes=[pltpu.VMEM((TM_MIX, SGU_WIDTH), F32),
                        pltpu.VMEM((FOX_WIDTH, D_MODEL), BF16),
                        pltpu.VMEM((SGU_WIDTH, D_MODEL), BF16),
                        pltpu.VMEM((D_MODEL, D_MODEL), BF16)],
        compiler_params=pltpu.CompilerParams(
            dimension_semantics=("arbitrary", "arbitrary"), vmem_limit_bytes=VMEM_LIMIT),
        name="fox_mix",
    )(x, aT, gpre, wug, gsgu, bsgu, ws, bsp, wa, wb, wout, gpost)


def _ffn_kernel(x_ref, gpre_ref, win_ref, wdown_ref, gpost_ref, o_ref):
    x = x_ref[...]
    h = _rms(x, gpre_ref[...]).astype(BF16)
    ff = jnp.zeros((TM_FFN, D_MODEL), F32)
    for lo, hi in zip(FF_EDGES[:-1], FF_EDGES[1:]):
        g = jnp.dot(h, win_ref[:, lo:hi], preferred_element_type=F32)
        u = jnp.dot(h, win_ref[:, D_FF + lo:D_FF + hi], preferred_element_type=F32)
        a = (jax.nn.silu(g) * u).astype(BF16)
        ff = ff + jnp.dot(a, wdown_ref[lo:hi, :], preferred_element_type=F32)
    o_ref[...] = x + _rms(ff, gpost_ref[...])


def _ffn_call(x, gpre, win, wdown, gpost):
    n, _ = x.shape
    const = lambda shape: pl.BlockSpec(shape, lambda i: (0,) * len(shape))
    return pl.pallas_call(
        _ffn_kernel,
        grid=(n // TM_FFN,),
        in_specs=[
            pl.BlockSpec((TM_FFN, D_MODEL), lambda i: (i, 0)),
            const((1, D_MODEL)),
            const((D_MODEL, 2 * D_FF)),
            const((D_FF, D_MODEL)),
            const((1, D_MODEL)),
        ],
        out_specs=pl.BlockSpec((TM_FFN, D_MODEL), lambda i: (i, 0)),
        out_shape=jax.ShapeDtypeStruct(x.shape, F32),
        compiler_params=pltpu.CompilerParams(
            dimension_semantics=("arbitrary",), vmem_limit_bytes=VMEM_LIMIT),
        name="fox_ffn",
    )(x, gpre, win, wdown, gpost)


def _qkv_stage(x, g_pre, wqkv, wfT, b_forget, g_q, g_k):
    bf = jnp.broadcast_to(b_forget[:, None], (HEADS, LANES))
    gq = jnp.broadcast_to(jnp.tile(g_q, HEADS)[:, None], (FOX_WIDTH, LANES))
    gk = jnp.broadcast_to(jnp.tile(g_k, HEADS)[:, None], (FOX_WIDTH, LANES))
    return _qkv_call(x, g_pre[None, :], wqkv, wfT, bf, gq, gk)


def kernel(x, g_pre_mix, w_in, b_forget, g_q, g_k, g_sgu, b_sgu, w_spatial, b_spatial,
           w_branch_a, w_branch_b, w_out, g_post_mix, g_pre_ffn, w_ffn_in, w_ffn_down,
           g_post_ffn):
    bsz, s_len, _ = x.shape
    for layer in range(g_pre_mix.shape[0]):
        wqkv, wfT, wug = _win_prep_call(jnp.transpose(w_in[layer]))
        bsp = jnp.repeat(jnp.transpose(b_spatial[layer]), SGU_WIDTH // SGU_GROUPS, axis=1)

        qT, k, vT = _qkv_stage(x, g_pre_mix[layer], wqkv, wfT, b_forget[layer], g_q[layer],
                               g_k[layer])
        aT, wfin, wfdn = _attn_call(qT, k, vT, w_ffn_in, w_ffn_down, layer)
        x = _mix_call(x, aT, g_pre_mix[layer][None, :], wug, g_sgu[layer][None, :],
                      b_sgu[layer][None, :], w_spatial[layer], bsp,
                      w_branch_a, w_branch_b, w_out, layer, g_post_mix[layer][None, :])
        x = _ffn_call(x.reshape(bsz * s_len, D_MODEL), g_pre_ffn[layer][None, :], wfin, wfdn,
                      g_post_ffn[layer][None, :]).reshape(bsz, s_len, D_MODEL)
    return x
```

```python
import jax
import jax.numpy as jnp
from jax import lax
from jax.experimental import pallas as pl
from jax.experimental.pallas import tpu as pltpu

F32 = jnp.float32
BF16 = jnp.bfloat16

D_MODEL = 1024
CHUNK = 64
HEAD_DIM = 64
FOX_WIDTH = 512
HEADS = 8
SGU_WIDTH = 512
SGU_GROUPS = 8
SGU_WINDOW = 128
D_FF = 2816
EPS = 1e-6

Q_OFF = 0
K_OFF = 512
V_OFF = 1024
F_OFF = 1536
U_OFF = 1544
G_OFF = 2568

LANES = 128
HEAD_PAD = LANES
N_SPLIT = 3

TM_QKV = 512
TQ = TM_QKV
TK = TM_QKV
HALF = TK // 2
N_Q = 4096 // TQ
HPS = 2
LAG2 = 2
LAG3 = 4
SLOT_ORDER = (3, 2, 1)
RING = 3
V_ROWS = HEAD_DIM + 16
LOG2E = 1.4426950408889634
TM_MIX = 512
TM_FFN = 512
MXU_DIM = 256
FF_EDGES = (0, 6 * MXU_DIM, D_FF)
NEG = -1e30

VMEM_LIMIT = 56 * 1024 * 1024


def _rms(x, g):
    ms = jnp.mean(x * x, axis=-1, keepdims=True)
    return x * lax.rsqrt(ms + EPS) * g


def _split3(d):
    d1 = d.astype(BF16)
    r1 = d - d1.astype(F32)
    d2 = r1.astype(BF16)
    d3 = (r1 - d2.astype(F32)).astype(BF16)
    return d1, d2, d3


def _win_prep_kernel(wT_ref, wqkv_ref, wfT_ref, wug_ref):
    wqkv_ref[...] = wT_ref[Q_OFF:F_OFF, :].T.astype(BF16)
    wug_ref[...] = wT_ref[U_OFF:, :].T.astype(BF16)
    wfT_ref[...] = wT_ref[F_OFF:F_OFF + 2 * HEADS, :].astype(BF16)


def _win_prep_call(wT):
    cols, d = wT.shape
    chunk = d // 4
    return pl.pallas_call(
        _win_prep_kernel,
        grid=(d // chunk,),
        in_specs=[pl.BlockSpec((cols, chunk), lambda i: (0, i))],
        out_specs=[pl.BlockSpec((chunk, F_OFF), lambda i: (i, 0)),
                   pl.BlockSpec((2 * HEADS, chunk), lambda i: (0, i)),
                   pl.BlockSpec((chunk, cols - U_OFF), lambda i: (i, 0))],
        out_shape=[jax.ShapeDtypeStruct((d, F_OFF), BF16),
                   jax.ShapeDtypeStruct((2 * HEADS, d), BF16),
                   jax.ShapeDtypeStruct((d, cols - U_OFF), BF16)],
        compiler_params=pltpu.CompilerParams(
            dimension_semantics=("arbitrary",), vmem_limit_bytes=VMEM_LIMIT),
        name="fox_win_prep",
    )(wT)


def _qkv_kernel(x_ref, gpre_ref, wqkv_ref, wf_ref, bf_ref, gq_ref, gk_ref,
                qT_ref, k_ref, vT_ref, carry_ref):
    @pl.when(pl.program_id(1) == 0)
    def _():
        carry_ref[...] = jnp.zeros_like(carry_ref)

    x = x_ref[0]
    h = _rms(x, gpre_ref[...]).astype(BF16)

    f = lax.dot_general(wf_ref[...], h, (((1,), (1,)), ((), ())),
                        preferred_element_type=F32)[:HEADS, :]
    f = f + jnp.concatenate([bf_ref[...]] * (TM_QKV // LANES), axis=1)
    dT = jnp.minimum(f, 0.0) - jnp.log1p(jnp.exp(-jnp.abs(f)))
    lane = lax.broadcasted_iota(jnp.int32, dT.shape, 1)
    shift = 1
    while shift < TM_QKV:
        dT = dT + jnp.where(lane >= shift, pltpu.roll(dT, shift, 1), 0.0)
        shift *= 2
    dT = dT + jnp.concatenate([carry_ref[...]] * (TM_QKV // LANES), axis=1)
    carry_ref[...] = jnp.broadcast_to(dT[:, TM_QKV - 1:TM_QKV], carry_ref.shape)
    d1, d2, d3 = (p.astype(F32) for p in _split3(dT * LOG2E))

    sub = lax.broadcasted_iota(jnp.int32, (8, TM_QKV), 0)
    pad = jnp.zeros((HEAD_PAD - HEAD_DIM - 8, TM_QKV), F32)

    def head_aug_t(cols, g_ref, g_scale, hd, is_q):
        rows = cols[hd * HEAD_DIM:(hd + 1) * HEAD_DIM, :]
        g = jnp.concatenate([g_ref[hd * HEAD_DIM:(hd + 1) * HEAD_DIM, :] * g_scale]
                            * (TM_QKV // LANES), axis=1)
        ms = jnp.mean(rows * rows, axis=0, keepdims=True)
        normed = rows * lax.rsqrt(ms + EPS) * g
        sgn = 1.0 if is_q else -1.0
        pieces = jnp.where(sub % N_SPLIT == 0, d1[hd:hd + 1],
                           jnp.where(sub % N_SPLIT == 1, d2[hd:hd + 1], d3[hd:hd + 1])) * sgn
        d_rows = (sub < N_SPLIT) if is_q else ((sub >= N_SPLIT) & (sub < 2 * N_SPLIT))
        aug = jnp.where(d_rows, pieces, jnp.where(sub < 2 * N_SPLIT, 1.0, 0.0))
        return jnp.concatenate([normed, aug, pad], axis=0)

    qT = jnp.dot(h, wqkv_ref[:, Q_OFF:K_OFF], preferred_element_type=F32).T
    for hd in range(HEADS):
        qT_ref[0, hd, 0] = head_aug_t(qT, gq_ref, HEAD_DIM ** -0.5 * LOG2E, hd, True).astype(BF16)
    kT = jnp.dot(h, wqkv_ref[:, K_OFF:V_OFF], preferred_element_type=F32).T
    for hd in range(HEADS):
        k_ref[0, :, hd * HEAD_PAD:(hd + 1) * HEAD_PAD] = (
            head_aug_t(kT, gk_ref, 1.0, hd, False).T.astype(BF16))

    vT = jnp.dot(h, wqkv_ref[:, V_OFF:F_OFF], preferred_element_type=F32).T
    ones_rows = (lax.broadcasted_iota(jnp.int32, (V_ROWS - HEAD_DIM, TM_QKV), 0) == 0).astype(BF16)
    for hd in range(HEADS):
        vT_ref[0, hd, 0, :HEAD_DIM, :] = vT[hd * HEAD_DIM:(hd + 1) * HEAD_DIM, :].astype(BF16)
        vT_ref[0, hd, 0, HEAD_DIM:, :] = ones_rows


def _qkv_call(x, gpre, wqkv, wf, bf, gq, gk):
    b, s, _ = x.shape
    n_s = s // TM_QKV
    const = lambda shape: pl.BlockSpec(shape, lambda bi, si: (0,) * len(shape))
    return pl.pallas_call(
        _qkv_kernel,
        grid=(b, n_s),
        in_specs=[
            pl.BlockSpec((1, TM_QKV, D_MODEL), lambda bi, si: (bi, si, 0)),
            const((1, D_MODEL)),
            const((D_MODEL, 3 * FOX_WIDTH)),
            const((2 * HEADS, D_MODEL)),
            const((HEADS, LANES)),
            const((FOX_WIDTH, LANES)),
            const((FOX_WIDTH, LANES)),
        ],
        out_specs=[
            pl.BlockSpec((1, HEADS, 1, HEAD_PAD, TM_QKV), lambda bi, si: (bi, 0, si, 0, 0)),
            pl.BlockSpec((1, TM_QKV, HEADS * HEAD_PAD), lambda bi, si: (bi, si, 0)),
            pl.BlockSpec((1, HEADS, 1, V_ROWS, TM_QKV), lambda bi, si: (bi, 0, si, 0, 0)),
        ],
        out_shape=[
            jax.ShapeDtypeStruct((b, HEADS, n_s, HEAD_PAD, TM_QKV), BF16),
            jax.ShapeDtypeStruct((b, s, HEADS * HEAD_PAD), BF16),
            jax.ShapeDtypeStruct((b, HEADS, n_s, V_ROWS, TM_QKV), BF16),
        ],
        scratch_shapes=[pltpu.VMEM((HEADS, LANES), F32)],
        compiler_params=pltpu.CompilerParams(
            dimension_semantics=("arbitrary", "arbitrary"), vmem_limit_bytes=VMEM_LIMIT),
        name="fox_qkv",
    )(x, gpre, wqkv, wf, bf, gq, gk)


def _attn_kernel(qT_ref, k_ref, vT_ref, wfin32_ref, wfdn32_ref, o_ref, wfin_ref, wfdn_ref,
                 *scratch):
    wfin_ref[...] = wfin32_ref[0].astype(BF16)
    wfdn_ref[...] = wfdn32_ref[0].astype(BF16)

    sb = scratch[:RING]
    pb = scratch[RING:2 * RING]
    m_ref, acc_ref, bias_ref = scratch[2 * RING:]

    @pl.when((pl.program_id(0) == 0) & (pl.program_id(1) == 0))
    def _():
        row = lax.broadcasted_iota(jnp.int32, (TK, TQ), 0)
        col = lax.broadcasted_iota(jnp.int32, (TK, TQ), 1)
        bias_ref[...] = jnp.where(row <= col, 0.0, NEG)


    def s1(t, r, diag):
        qi, j = t
        start = j * TK if isinstance(j, int) else pl.multiple_of(j * TK, TK)
        out = []
        for hh in range(HPS):
            lanes = slice(hh * HEAD_PAD, (hh + 1) * HEAD_PAD)
            if diag:
                top = jnp.dot(k_ref[0, pl.ds(start, HALF), lanes], qT_ref[0, hh, qi],
                              preferred_element_type=F32) + bias_ref[:HALF, :]
                bot = jnp.dot(k_ref[0, pl.ds(start + HALF, HALF), lanes],
                              qT_ref[0, hh, qi, :, HALF:],
                              preferred_element_type=F32) + bias_ref[HALF:, HALF:]
                sb[r][hh, :HALF, :] = top
                sb[r][hh, HALF:, HALF:] = bot
                top_max = jnp.max(top, axis=0, keepdims=True)
                right = jnp.maximum(top_max[:, HALF:], jnp.max(bot, axis=0, keepdims=True))
                m_before = jnp.full((1, TQ), NEG, F32)
                m_after = jnp.concatenate([top_max[:, :HALF], right], axis=1)
            else:
                s = jnp.dot(k_ref[0, pl.ds(start, TK), lanes], qT_ref[0, hh, qi],
                            preferred_element_type=F32)
                sb[r][hh] = s
                m_before = m_ref[hh, qi]
                m_after = jnp.maximum(m_before, jnp.max(s, axis=0, keepdims=True))
            m_ref[hh, qi] = m_after
            out.append((m_before, m_after))
        return tuple(out)

    def s2(r, ms, diag):
        out = []
        for hh, (m_before, m_after) in enumerate(ms):
            if diag:
                pb[r][hh, :HALF, :] = jnp.exp2(sb[r][hh, :HALF, :] - m_after).astype(BF16)
                pb[r][hh, HALF:, HALF:] = jnp.exp2(
                    sb[r][hh, HALF:, HALF:] - m_after[:, HALF:]).astype(BF16)
            else:
                pb[r][hh] = jnp.exp2(sb[r][hh] - m_after).astype(BF16)
            out.append(jnp.exp2(m_before - m_after))
        return tuple(out)

    def s3(t, r, resc, diag):
        qi, j = t
        for hh, a in enumerate(resc):
            if diag:
                left = jnp.dot(vT_ref[0, hh, j, :, :HALF], pb[r][hh, :HALF, :HALF],
                               preferred_element_type=F32)
                right = jnp.dot(vT_ref[0, hh, j], pb[r][hh, :, HALF:],
                                preferred_element_type=F32)
                acc_ref[hh, qi] = jnp.concatenate([left, right], axis=1)
            else:
                pv = jnp.dot(vT_ref[0, hh, j], pb[r][hh], preferred_element_type=F32)
                acc_ref[hh, qi] = a * acc_ref[hh, qi] + pv

    def pipeline(tiles, n_diag, successor):
        n_tiles = len(tiles)
        ms = {}
        resc = {}

        def run_slot(a1, a2, a3):
            out = {}
            for stage in SLOT_ORDER:
                if stage == 1 and a1 is not None:
                    out[1] = s1(*a1)
                if stage == 2 and a2 is not None:
                    out[2] = s2(*a2)
                if stage == 3 and a3 is not None:
                    s3(*a3)
            return out.get(1), out.get(2)

        def slot(u):
            n2 = u - LAG2
            n3 = u - LAG3
            a1 = (tiles[u], u % RING, u < n_diag) if u < n_tiles else None
            a2 = (n2 % RING, ms.pop(n2), n2 < n_diag) if 0 <= n2 < n_tiles else None
            a3 = (tiles[n3], n3 % RING, resc.pop(n3), n3 < n_diag) if 0 <= n3 < n_tiles else None
            ms_out, resc_out = run_slot(a1, a2, a3)
            if a1 is not None:
                ms[u] = ms_out
            if a2 is not None:
                resc[n2] = resc_out

        loop_lo = n_diag + LAG3
        n_iter = (n_tiles - loop_lo) // RING
        loop_hi = loop_lo + n_iter * RING
        for u in range(loop_lo):
            slot(u)

        def body(_, carry):
            hist, ms_q, resc_q = carry
            for i in range(RING):
                hist = hist[1:] + (successor(hist[-1]),)
                u = loop_lo + i
                ms_out, resc_out = run_slot((hist[-1], u % RING, False),
                                            ((u - LAG2) % RING, ms_q[0], False),
                                            (hist[0], (u - LAG3) % RING, resc_q[0], False))
                resc_q = resc_q[1:] + (resc_out,)
                ms_q = ms_q[1:] + (ms_out,)
            return hist, ms_q, resc_q

        if n_iter > 0:
            hist0 = tuple((jnp.int32(q), jnp.int32(j))
                          for q, j in tiles[loop_lo - LAG3 - 1:loop_lo])
            ms0 = tuple(ms.pop(n) for n in range(loop_lo - LAG2, loop_lo))
            resc0 = tuple(resc.pop(n) for n in range(loop_lo - LAG3, loop_lo - LAG2))
            _, ms1, resc1 = lax.fori_loop(0, n_iter, body, (hist0, ms0, resc0))
            ms.update(zip(range(loop_hi - LAG2, loop_hi), ms1))
            resc.update(zip(range(loop_hi - LAG3, loop_hi - LAG2), resc1))
        for u in range(loop_hi, n_tiles + LAG3):
            slot(u)

    def next_full(t):
        qi, j = t
        wrap = j + 1 == qi
        return jnp.where(wrap, qi + 1, qi), jnp.where(wrap, 0, j + 1)

    pipeline([(q, q) for q in range(N_Q)] + [(q, j) for q in range(1, N_Q) for j in range(q)],
             N_Q, next_full)

    for hh in range(HPS):
        for qi in range(N_Q):
            acc = acc_ref[hh, qi]
            o_ref[0, hh, qi] = (acc[:HEAD_DIM] / acc[HEAD_DIM:HEAD_DIM + 1]).astype(BF16)


def _attn_call(qT, k, vT, w_ffn_in, w_ffn_down, layer):
    b = qT.shape[0]
    s = k.shape[1]
    n_hg = HEADS // HPS
    steps = b * n_hg
    _, d, ff2 = w_ffn_in.shape
    _, ff, _ = w_ffn_down.shape
    slab = lambda bi, hi: (bi * n_hg + hi, 0)
    return pl.pallas_call(
        _attn_kernel,
        grid=(b, n_hg),
        in_specs=[
            pl.BlockSpec((1, HPS, N_Q, HEAD_PAD, TQ), lambda bi, hi: (bi, hi, 0, 0, 0)),
            pl.BlockSpec((1, s, HPS * HEAD_PAD), lambda bi, hi: (bi, 0, hi)),
            pl.BlockSpec((1, HPS, N_Q, V_ROWS, TK), lambda bi, hi: (bi, hi, 0, 0, 0)),
            pl.BlockSpec((1, d // steps, ff2), lambda bi, hi: (layer,) + slab(bi, hi)),
            pl.BlockSpec((1, ff // steps, d), lambda bi, hi: (layer,) + slab(bi, hi)),
        ],
        out_specs=[
            pl.BlockSpec((1, HPS, N_Q, HEAD_DIM, TQ), lambda bi, hi: (bi, hi, 0, 0, 0)),
            pl.BlockSpec((d // steps, ff2), slab),
            pl.BlockSpec((ff // steps, d), slab),
        ],
        out_shape=[
            jax.ShapeDtypeStruct((b, HEADS, N_Q, HEAD_DIM, TQ), BF16),
            jax.ShapeDtypeStruct((d, ff2), BF16),
            jax.ShapeDtypeStruct((ff, d), BF16),
        ],
        scratch_shapes=(
            [pltpu.VMEM((HPS, TK, TQ), F32)] * RING + [pltpu.VMEM((HPS, TK, TQ), BF16)] * RING
            + [pltpu.VMEM((HPS, N_Q, 1, TQ), F32), pltpu.VMEM((HPS, N_Q, V_ROWS, TQ), F32),
               pltpu.VMEM((TK, TQ), F32)]),
        compiler_params=pltpu.CompilerParams(
            dimension_semantics=("arbitrary", "arbitrary"),
            vmem_limit_bytes=VMEM_LIMIT),
        name="fox_attn",
    )(qT, k, vT, w_ffn_in, w_ffn_down)


def _mix_kernel(x_ref, aT_ref, gpre_ref, wug_ref, gsgu_ref, bsgu_ref, ws_ref, bsp_ref,
                wa32_ref, wb32_ref, wout32_ref, gpost_ref, o_ref, mixed_ref,
                wa_ref, wb_ref, wout_ref):
    @pl.when((pl.program_id(0) == 0) & (pl.program_id(1) == 0))
    def _():
        wa_ref[...] = wa32_ref[0].astype(BF16)
        wb_ref[...] = wb32_ref[0].astype(BF16)
        wout_ref[...] = wout32_ref[0].astype(BF16)

    x = x_ref[0]
    h = _rms(x, gpre_ref[...]).astype(BF16)
    uv = jax.nn.gelu(jnp.dot(h, wug_ref[:, :2 * SGU_WIDTH], preferred_element_type=F32))
    u = uv[:, :SGU_WIDTH]
    v = uv[:, SGU_WIDTH:]
    mu = jnp.mean(v, axis=-1, keepdims=True)
    vc = v - mu
    var = jnp.mean(vc * vc, axis=-1, keepdims=True)
    vn = vc * lax.rsqrt(var + EPS) * gsgu_ref[...] + bsgu_ref[...]

    t_idx = lax.broadcasted_iota(jnp.int32, (SGU_WINDOW, SGU_WINDOW), 0)
    s_idx = lax.broadcasted_iota(jnp.int32, (SGU_WINDOW, SGU_WINDOW), 1)
    wmask = (s_idx // CHUNK) <= (t_idx // CHUNK)
    lo = lax.broadcasted_iota(jnp.int32, (SGU_WINDOW, LANES), 1) < (SGU_WIDTH // SGU_GROUPS)
    for gp in range(SGU_GROUPS // 2):
        w_pair = jnp.concatenate(
            [jnp.where(wmask, ws_ref[2 * gp], 0.0), jnp.where(wmask, ws_ref[2 * gp + 1], 0.0)],
            axis=1).astype(BF16)
        for w in range(TM_MIX // SGU_WINDOW):
            vp = vn[w * SGU_WINDOW:(w + 1) * SGU_WINDOW, gp * LANES:(gp + 1) * LANES]
            rhs = jnp.concatenate([jnp.where(lo, vp, 0.0), jnp.where(lo, 0.0, vp)],
                                  axis=0).astype(BF16)
            mixed_ref[w * SGU_WINDOW:(w + 1) * SGU_WINDOW, gp * LANES:(gp + 1) * LANES] = (
                jnp.dot(w_pair, rhs, preferred_element_type=F32)
                + bsp_ref[:, gp * LANES:(gp + 1) * LANES])
    sgu = (u * mixed_ref[...]).astype(BF16)

    y_b = jnp.dot(sgu, wb_ref[...], preferred_element_type=F32)
    aT = aT_ref[0, :, 0].reshape(FOX_WIDTH, TM_MIX)
    y_a = lax.dot_general(aT, wa_ref[...], (((0,), (0,)), ((), ())),
                          preferred_element_type=F32)
    gates = jax.nn.sigmoid(jnp.dot(h, wug_ref[:, 2 * SGU_WIDTH:], preferred_element_type=F32))
    merged = (gates[:, :D_MODEL] * y_a + gates[:, D_MODEL:] * y_b).astype(BF16)
    o = jnp.dot(merged, wout_ref[...], preferred_element_type=F32)
    o_ref[0] = x + _rms(o, gpost_ref[...])


def _mix_call(x, aT, gpre, wug, gsgu, bsgu, ws, bsp, wa, wb, wout, layer, gpost):
    b, s, _ = x.shape
    const = lambda shape: pl.BlockSpec(shape, lambda bi, si: (0,) * len(shape))
    layer_w = lambda shape: pl.BlockSpec((1,) + shape, lambda bi, si: (layer, 0, 0),
                                         pipeline_mode=pl.Buffered(1))
    return pl.pallas_call(
        _mix_kernel,
        grid=(b, s // TM_MIX),
        in_specs=[
            pl.BlockSpec((1, TM_MIX, D_MODEL), lambda bi, si: (bi, si, 0)),
            pl.BlockSpec((1, HEADS, 1, HEAD_DIM, TM_MIX),
                         lambda bi, si: (bi, 0, si // (TQ // TM_MIX), 0, si % (TQ // TM_MIX))),
            const((1, D_MODEL)),
            const((D_MODEL, 2 * SGU_WIDTH + 2 * D_MODEL)),
            const((1, SGU_WIDTH)),
            const((1, SGU_WIDTH)),
            const((SGU_GROUPS, SGU_WINDOW, SGU_WINDOW)),
            const((SGU_WINDOW, SGU_WIDTH)),
            layer_w((FOX_WIDTH, D_MODEL)),
            layer_w((SGU_WIDTH, D_MODEL)),
            layer_w((D_MODEL, D_MODEL)),
            const((1, D_MODEL)),
        ],
        out_specs=pl.BlockSpec((1, TM_MIX, D_MODEL), lambda bi, si: (bi, si, 0)),
        out_shape=jax.ShapeDtypeStruct(x.shape, F32),
        scratch_shapes=[pltpu.VMEM((TM_MIX, SGU_WIDTH), F32),
                        pltpu.VMEM((FOX_WIDTH, D_MODEL), BF16),
                        pltpu.VMEM((SGU_WIDTH, D_MODEL), BF16),
                        pltpu.VMEM((D_MODEL, D_MODEL), BF16)],
        compiler_params=pltpu.CompilerParams(
            dimension_semantics=("arbitrary", "arbitrary"), vmem_limit_bytes=VMEM_LIMIT),
        name="fox_mix",
    )(x, aT, gpre, wug, gsgu, bsgu, ws, bsp, wa, wb, wout, gpost)


def _ffn_kernel(x_ref, gpre_ref, win_ref, wdown_ref, gpost_ref, o_ref):
    x = x_ref[...]
    h = _rms(x, gpre_ref[...]).astype(BF16)
    ff = jnp.zeros((TM_FFN, D_MODEL), F32)
    for lo, hi in zip(FF_EDGES[:-1], FF_EDGES[1:]):
        g = jnp.dot(h, win_ref[:, lo:hi], preferred_element_type=F32)
        u = jnp.dot(h, win_ref[:, D_FF + lo:D_FF + hi], preferred_element_type=F32)
        a = (jax.nn.silu(g) * u).astype(BF16)
        ff = ff + jnp.dot(a, wdown_ref[lo:hi, :], preferred_element_type=F32)
    o_ref[...] = x + _rms(ff, gpost_ref[...])


def _ffn_call(x, gpre, win, wdown, gpost):
    n, _ = x.shape
    const = lambda shape: pl.BlockSpec(shape, lambda i: (0,) * len(shape))
    return pl.pallas_call(
        _ffn_kernel,
        grid=(n // TM_FFN,),
        in_specs=[
            pl.BlockSpec((TM_FFN, D_MODEL), lambda i: (i, 0)),
            const((1, D_MODEL)),
            const((D_MODEL, 2 * D_FF)),
            const((D_FF, D_MODEL)),
            const((1, D_MODEL)),
        ],
        out_specs=pl.BlockSpec((TM_FFN, D_MODEL), lambda i: (i, 0)),
        out_shape=jax.ShapeDtypeStruct(x.shape, F32),
        compiler_params=pltpu.CompilerParams(
            dimension_semantics=("arbitrary",), vmem_limit_bytes=VMEM_LIMIT),
        name="fox_ffn",
    )(x, gpre, win, wdown, gpost)


def _qkv_stage(x, g_pre, wqkv, wfT, b_forget, g_q, g_k):
    bf = jnp.broadcast_to(b_forget[:, None], (HEADS, LANES))
    gq = jnp.broadcast_to(jnp.tile(g_q, HEADS)[:, None], (FOX_WIDTH, LANES))
    gk = jnp.broadcast_to(jnp.tile(g_k, HEADS)[:, None], (FOX_WIDTH, LANES))
    return _qkv_call(x, g_pre[None, :], wqkv, wfT, bf, gq, gk)


def kernel(x, g_pre_mix, w_in, b_forget, g_q, g_k, g_sgu, b_sgu, w_spatial, b_spatial,
           w_branch_a, w_branch_b, w_out, g_post_mix, g_pre_ffn, w_ffn_in, w_ffn_down,
           g_post_ffn):
    bsz, s_len, _ = x.shape
    for layer in range(g_pre_mix.shape[0]):
        wqkv, wfT, wug = _win_prep_call(jnp.transpose(w_in[layer]))
        bsp = jnp.repeat(jnp.transpose(b_spatial[layer]), SGU_WIDTH // SGU_GROUPS, axis=1)

        qT, k, vT = _qkv_stage(x, g_pre_mix[layer], wqkv, wfT, b_forget[layer], g_q[layer],
                               g_k[layer])
        aT, wfin, wfdn = _attn_call(qT, k, vT, w_ffn_in, w_ffn_down, layer)
        x = _mix_call(x, aT, g_pre_mix[layer][None, :], wug, g_sgu[layer][None, :],
                      b_sgu[layer][None, :], w_spatial[layer], bsp,
                      w_branch_a, w_branch_b, w_out, layer, g_post_mix[layer][None, :])
        x = _ffn_call(x.reshape(bsz * s_len, D_MODEL), g_pre_ffn[layer][None, :], wfin, wfdn,
                      g_post_ffn[layer][None, :]).reshape(bsz, s_len, D_MODEL)
    return x
```

```python
import jax
import jax.numpy as jnp
from jax import lax
from jax.experimental import pallas as pl
from jax.experimental.pallas import tpu as pltpu

F32 = jnp.float32
BF16 = jnp.bfloat16

D_MODEL = 1024
CHUNK = 64
HEAD_DIM = 64
FOX_WIDTH = 512
HEADS = 8
SGU_WIDTH = 512
SGU_GROUPS = 8
SGU_WINDOW = 128
D_FF = 2816
EPS = 1e-6

Q_OFF = 0
K_OFF = 512
V_OFF = 1024
F_OFF = 1536
U_OFF = 1544
G_OFF = 2568

LANES = 128
HEAD_PAD = LANES
N_SPLIT = 3

TM_QKV = 512
TQ = TM_QKV
TK = TM_QKV
HALF = TK // 2
N_Q = 4096 // TQ
HPS = 2
LAG2 = 2
LAG3 = 4
SLOT_ORDER = (3, 2, 1)
RING = 3
V_ROWS = HEAD_DIM + 16
LOG2E = 1.4426950408889634
TM_MIX = 512
TM_FFN = 512
MXU_DIM = 256
FF_EDGES = (0, 6 * MXU_DIM, D_FF)
NEG = -1e30

VMEM_LIMIT = 56 * 1024 * 1024


def _rms(x, g):
    ms = jnp.mean(x * x, axis=-1, keepdims=True)
    return x * lax.rsqrt(ms + EPS) * g


def _split3(d):
    d1 = d.astype(BF16)
    r1 = d - d1.astype(F32)
    d2 = r1.astype(BF16)
    d3 = (r1 - d2.astype(F32)).astype(BF16)
    return d1, d2, d3


def _win_prep_kernel(wT_ref, wqkv_ref, wfT_ref):
    wqkv_ref[...] = wT_ref[Q_OFF:F_OFF, :].T.astype(BF16)
    wfT_ref[...] = wT_ref[F_OFF:F_OFF + 2 * HEADS, :].astype(BF16)


def _win_prep_call(wT):
    _, d = wT.shape
    chunk = d // 4
    return pl.pallas_call(
        _win_prep_kernel,
        grid=(d // chunk,),
        in_specs=[pl.BlockSpec((F_OFF + 2 * HEADS, chunk), lambda i: (0, i))],
        out_specs=[pl.BlockSpec((chunk, F_OFF), lambda i: (i, 0)),
                   pl.BlockSpec((2 * HEADS, chunk), lambda i: (0, i))],
        out_shape=[jax.ShapeDtypeStruct((d, F_OFF), BF16),
                   jax.ShapeDtypeStruct((2 * HEADS, d), BF16)],
        compiler_params=pltpu.CompilerParams(
            dimension_semantics=("arbitrary",), vmem_limit_bytes=VMEM_LIMIT),
        name="fox_win_prep",
    )(wT)


def _qkv_kernel(x_ref, gpre_ref, wqkv_ref, wf_ref, bf_ref, gq_ref, gk_ref,
                qT_ref, k_ref, vT_ref, carry_ref):
    @pl.when(pl.program_id(1) == 0)
    def _():
        carry_ref[...] = jnp.zeros_like(carry_ref)

    x = x_ref[0]
    h = _rms(x, gpre_ref[...]).astype(BF16)

    f = lax.dot_general(wf_ref[...], h, (((1,), (1,)), ((), ())),
                        preferred_element_type=F32)[:HEADS, :]
    f = f + jnp.concatenate([bf_ref[...]] * (TM_QKV // LANES), axis=1)
    dT = jnp.minimum(f, 0.0) - jnp.log1p(jnp.exp(-jnp.abs(f)))
    lane = lax.broadcasted_iota(jnp.int32, dT.shape, 1)
    shift = 1
    while shift < TM_QKV:
        dT = dT + jnp.where(lane >= shift, pltpu.roll(dT, shift, 1), 0.0)
        shift *= 2
    dT = dT + jnp.concatenate([carry_ref[...]] * (TM_QKV // LANES), axis=1)
    carry_ref[...] = jnp.broadcast_to(dT[:, TM_QKV - 1:TM_QKV], carry_ref.shape)
    d1, d2, d3 = (p.astype(F32) for p in _split3(dT * LOG2E))

    sub = lax.broadcasted_iota(jnp.int32, (8, TM_QKV), 0)
    pad = jnp.zeros((HEAD_PAD - HEAD_DIM - 8, TM_QKV), F32)

    def head_aug_t(cols, g_ref, g_scale, hd, is_q):
        rows = cols[hd * HEAD_DIM:(hd + 1) * HEAD_DIM, :]
        g = jnp.concatenate([g_ref[hd * HEAD_DIM:(hd + 1) * HEAD_DIM, :] * g_scale]
                            * (TM_QKV // LANES), axis=1)
        ms = jnp.mean(rows * rows, axis=0, keepdims=True)
        normed = rows * lax.rsqrt(ms + EPS) * g
        sgn = 1.0 if is_q else -1.0
        pieces = jnp.where(sub % N_SPLIT == 0, d1[hd:hd + 1],
                           jnp.where(sub % N_SPLIT == 1, d2[hd:hd + 1], d3[hd:hd + 1])) * sgn
        d_rows = (sub < N_SPLIT) if is_q else ((sub >= N_SPLIT) & (sub < 2 * N_SPLIT))
        aug = jnp.where(d_rows, pieces, jnp.where(sub < 2 * N_SPLIT, 1.0, 0.0))
        return jnp.concatenate([normed, aug, pad], axis=0)

    qT = jnp.dot(h, wqkv_ref[:, Q_OFF:K_OFF], preferred_element_type=F32).T
    for hd in range(HEADS):
        qT_ref[0, hd, 0] = head_aug_t(qT, gq_ref, HEAD_DIM ** -0.5 * LOG2E, hd, True).astype(BF16)
    kT = jnp.dot(h, wqkv_ref[:, K_OFF:V_OFF], preferred_element_type=F32).T
    for hd in range(HEADS):
        k_ref[0, :, hd * HEAD_PAD:(hd + 1) * HEAD_PAD] = (
            head_aug_t(kT, gk_ref, 1.0, hd, False).T.astype(BF16))

    vT = jnp.dot(h, wqkv_ref[:, V_OFF:F_OFF], preferred_element_type=F32).T
    ones_rows = (lax.broadcasted_iota(jnp.int32, (V_ROWS - HEAD_DIM, TM_QKV), 0) == 0).astype(BF16)
    for hd in range(HEADS):
        vT_ref[0, hd, 0, :HEAD_DIM, :] = vT[hd * HEAD_DIM:(hd + 1) * HEAD_DIM, :].astype(BF16)
        vT_ref[0, hd, 0, HEAD_DIM:, :] = ones_rows


def _qkv_call(x, gpre, wqkv, wf, bf, gq, gk):
    b, s, _ = x.shape
    n_s = s // TM_QKV
    const = lambda shape: pl.BlockSpec(shape, lambda bi, si: (0,) * len(shape))
    return pl.pallas_call(
        _qkv_kernel,
        grid=(b, n_s),
        in_specs=[
            pl.BlockSpec((1, TM_QKV, D_MODEL), lambda bi, si: (bi, si, 0)),
            const((1, D_MODEL)),
            const((D_MODEL, 3 * FOX_WIDTH)),
            const((2 * HEADS, D_MODEL)),
            const((HEADS, LANES)),
            const((FOX_WIDTH, LANES)),
            const((FOX_WIDTH, LANES)),
        ],
        out_specs=[
            pl.BlockSpec((1, HEADS, 1, HEAD_PAD, TM_QKV), lambda bi, si: (bi, 0, si, 0, 0)),
            pl.BlockSpec((1, TM_QKV, HEADS * HEAD_PAD), lambda bi, si: (bi, si, 0)),
            pl.BlockSpec((1, HEADS, 1, V_ROWS, TM_QKV), lambda bi, si: (bi, 0, si, 0, 0)),
        ],
        out_shape=[
            jax.ShapeDtypeStruct((b, HEADS, n_s, HEAD_PAD, TM_QKV), BF16),
            jax.ShapeDtypeStruct((b, s, HEADS * HEAD_PAD), BF16),
            jax.ShapeDtypeStruct((b, HEADS, n_s, V_ROWS, TM_QKV), BF16),
        ],
        scratch_shapes=[pltpu.VMEM((HEADS, LANES), F32)],
        compiler_params=pltpu.CompilerParams(
            dimension_semantics=("arbitrary", "arbitrary"), vmem_limit_bytes=VMEM_LIMIT),
        name="fox_qkv",
    )(x, gpre, wqkv, wf, bf, gq, gk)


def _attn_kernel(qT_ref, k_ref, vT_ref, wfin32_ref, wfdn32_ref, wT_ref,
                 o_ref, wfin_ref, wfdn_ref, wug_ref, *scratch):
    wfin_ref[...] = wfin32_ref[0].astype(BF16)
    wfdn_ref[...] = wfdn32_ref[0].astype(BF16)
    wug_ref[...] = wT_ref[U_OFF:, :].T.astype(BF16)

    sb = scratch[:RING]
    pb = scratch[RING:2 * RING]
    m_ref, acc_ref, bias_ref = scratch[2 * RING:]

    @pl.when((pl.program_id(0) == 0) & (pl.program_id(1) == 0))
    def _():
        row = lax.broadcasted_iota(jnp.int32, (TK, TQ), 0)
        col = lax.broadcasted_iota(jnp.int32, (TK, TQ), 1)
        bias_ref[...] = jnp.where(row <= col, 0.0, NEG)


    def s1(t, r, diag):
        qi, j = t
        start = j * TK if isinstance(j, int) else pl.multiple_of(j * TK, TK)
        out = []
        for hh in range(HPS):
            lanes = slice(hh * HEAD_PAD, (hh + 1) * HEAD_PAD)
            if diag:
                top = jnp.dot(k_ref[0, pl.ds(start, HALF), lanes], qT_ref[0, hh, qi],
                              preferred_element_type=F32) + bias_ref[:HALF, :]
                bot = jnp.dot(k_ref[0, pl.ds(start + HALF, HALF), lanes],
                              qT_ref[0, hh, qi, :, HALF:],
                              preferred_element_type=F32) + bias_ref[HALF:, HALF:]
                sb[r][hh, :HALF, :] = top
                sb[r][hh, HALF:, HALF:] = bot
                top_max = jnp.max(top, axis=0, keepdims=True)
                right = jnp.maximum(top_max[:, HALF:], jnp.max(bot, axis=0, keepdims=True))
                m_before = jnp.full((1, TQ), NEG, F32)
                m_after = jnp.concatenate([top_max[:, :HALF], right], axis=1)
            else:
                s = jnp.dot(k_ref[0, pl.ds(start, TK), lanes], qT_ref[0, hh, qi],
                            preferred_element_type=F32)
                sb[r][hh] = s
                m_before = m_ref[hh, qi]
                m_after = jnp.maximum(m_before, jnp.max(s, axis=0, keepdims=True))
            m_ref[hh, qi] = m_after
            out.append((m_before, m_after))
        return tuple(out)

    def s2(r, ms, diag):
        out = []
        for hh, (m_before, m_after) in enumerate(ms):
            if diag:
                pb[r][hh, :HALF, :] = jnp.exp2(sb[r][hh, :HALF, :] - m_after).astype(BF16)
                pb[r][hh, HALF:, HALF:] = jnp.exp2(
                    sb[r][hh, HALF:, HALF:] - m_after[:, HALF:]).astype(BF16)
            else:
                pb[r][hh] = jnp.exp2(sb[r][hh] - m_after).astype(BF16)
            out.append(jnp.exp2(m_before - m_after))
        return tuple(out)

    def s3(t, r, resc, diag):
        qi, j = t
        for hh, a in enumerate(resc):
            if diag:
                left = jnp.dot(vT_ref[0, hh, j, :, :HALF], pb[r][hh, :HALF, :HALF],
                               preferred_element_type=F32)
                right = jnp.dot(vT_ref[0, hh, j], pb[r][hh, :, HALF:],
                                preferred_element_type=F32)
                acc_ref[hh, qi] = jnp.concatenate([left, right], axis=1)
            else:
                pv = jnp.dot(vT_ref[0, hh, j], pb[r][hh], preferred_element_type=F32)
                acc_ref[hh, qi] = a * acc_ref[hh, qi] + pv

    def pipeline(tiles, n_diag, successor):
        n_tiles = len(tiles)
        ms = {}
        resc = {}

        def run_slot(a1, a2, a3):
            out = {}
            for stage in SLOT_ORDER:
                if stage == 1 and a1 is not None:
                    out[1] = s1(*a1)
                if stage == 2 and a2 is not None:
                    out[2] = s2(*a2)
                if stage == 3 and a3 is not None:
                    s3(*a3)
            return out.get(1), out.get(2)

        def slot(u):
            n2 = u - LAG2
            n3 = u - LAG3
            a1 = (tiles[u], u % RING, u < n_diag) if u < n_tiles else None
            a2 = (n2 % RING, ms.pop(n2), n2 < n_diag) if 0 <= n2 < n_tiles else None
            a3 = (tiles[n3], n3 % RING, resc.pop(n3), n3 < n_diag) if 0 <= n3 < n_tiles else None
            ms_out, resc_out = run_slot(a1, a2, a3)
            if a1 is not None:
                ms[u] = ms_out
            if a2 is not None:
                resc[n2] = resc_out

        loop_lo = n_diag + LAG3
        n_iter = (n_tiles - loop_lo) // RING
        loop_hi = loop_lo + n_iter * RING
        for u in range(loop_lo):
            slot(u)

        def body(_, carry):
            hist, ms_q, resc_q = carry
            for i in range(RING):
                hist = hist[1:] + (successor(hist[-1]),)
                u = loop_lo + i
                ms_out, resc_out = run_slot((hist[-1], u % RING, False),
                                            ((u - LAG2) % RING, ms_q[0], False),
                                            (hist[0], (u - LAG3) % RING, resc_q[0], False))
                resc_q = resc_q[1:] + (resc_out,)
                ms_q = ms_q[1:] + (ms_out,)
            return hist, ms_q, resc_q

        if n_iter > 0:
            hist0 = tuple((jnp.int32(q), jnp.int32(j))
                          for q, j in tiles[loop_lo - LAG3 - 1:loop_lo])
            ms0 = tuple(ms.pop(n) for n in range(loop_lo - LAG2, loop_lo))
            resc0 = tuple(resc.pop(n) for n in range(loop_lo - LAG3, loop_lo - LAG2))
            _, ms1, resc1 = lax.fori_loop(0, n_iter, body, (hist0, ms0, resc0))
            ms.update(zip(range(loop_hi - LAG2, loop_hi), ms1))
            resc.update(zip(range(loop_hi - LAG3, loop_hi - LAG2), resc1))
        for u in range(loop_hi, n_tiles + LAG3):
            slot(u)

    def next_full(t):
        qi, j = t
        wrap = j + 1 == qi
        return jnp.where(wrap, qi + 1, qi), jnp.where(wrap, 0, j + 1)

    pipeline([(q, q) for q in range(N_Q)] + [(q, j) for q in range(1, N_Q) for j in range(q)],
             N_Q, next_full)

    for hh in range(HPS):
        for qi in range(N_Q):
            acc = acc_ref[hh, qi]
            o_ref[0, hh, qi] = (acc[:HEAD_DIM] / acc[HEAD_DIM:HEAD_DIM + 1]).astype(BF16)


def _attn_call(qT, k, vT, w_ffn_in, w_ffn_down, layer, wT):
    b = qT.shape[0]
    s = k.shape[1]
    n_hg = HEADS // HPS
    steps = b * n_hg
    _, d, ff2 = w_ffn_in.shape
    _, ff, _ = w_ffn_down.shape
    cols = wT.shape[0]
    slab = lambda bi, hi: (bi * n_hg + hi, 0)
    hold = steps * LANES // d
    ug_slab = lambda bi, hi: (bi * n_hg + hi) // hold
    return pl.pallas_call(
        _attn_kernel,
        grid=(b, n_hg),
        in_specs=[
            pl.BlockSpec((1, HPS, N_Q, HEAD_PAD, TQ), lambda bi, hi: (bi, hi, 0, 0, 0)),
            pl.BlockSpec((1, s, HPS * HEAD_PAD), lambda bi, hi: (bi, 0, hi)),
            pl.BlockSpec((1, HPS, N_Q, V_ROWS, TK), lambda bi, hi: (bi, hi, 0, 0, 0)),
            pl.BlockSpec((1, d // steps, ff2), lambda bi, hi: (layer,) + slab(bi, hi)),
            pl.BlockSpec((1, ff // steps, d), lambda bi, hi: (layer,) + slab(bi, hi)),
            pl.BlockSpec((cols, LANES), lambda bi, hi: (0, ug_slab(bi, hi))),
        ],
        out_specs=[
            pl.BlockSpec((1, HPS, N_Q, HEAD_DIM, TQ), lambda bi, hi: (bi, hi, 0, 0, 0)),
            pl.BlockSpec((d // steps, ff2), slab),
            pl.BlockSpec((ff // steps, d), slab),
            pl.BlockSpec((LANES, cols - U_OFF), lambda bi, hi: (ug_slab(bi, hi), 0)),
        ],
        out_shape=[
            jax.ShapeDtypeStruct((b, HEADS, N_Q, HEAD_DIM, TQ), BF16),
            jax.ShapeDtypeStruct((d, ff2), BF16),
            jax.ShapeDtypeStruct((ff, d), BF16),
            jax.ShapeDtypeStruct((d, cols - U_OFF), BF16),
        ],
        scratch_shapes=(
            [pltpu.VMEM((HPS, TK, TQ), F32)] * RING + [pltpu.VMEM((HPS, TK, TQ), BF16)] * RING
            + [pltpu.VMEM((HPS, N_Q, 1, TQ), F32), pltpu.VMEM((HPS, N_Q, V_ROWS, TQ), F32),
               pltpu.VMEM((TK, TQ), F32)]),
        compiler_params=pltpu.CompilerParams(
            dimension_semantics=("arbitrary", "arbitrary"),
            vmem_limit_bytes=VMEM_LIMIT),
        name="fox_attn",
    )(qT, k, vT, w_ffn_in, w_ffn_down, wT)


def _mix_kernel(x_ref, aT_ref, gpre_ref, wug_ref, gsgu_ref, bsgu_ref, ws_ref, bsp_ref,
                wa32_ref, wb32_ref, wout32_ref, gpost_ref, o_ref, mixed_ref,
                wa_ref, wb_ref, wout_ref):
    @pl.when((pl.program_id(0) == 0) & (pl.program_id(1) == 0))
    def _():
        wa_ref[...] = wa32_ref[0].astype(BF16)
        wb_ref[...] = wb32_ref[0].astype(BF16)
        wout_ref[...] = wout32_ref[0].astype(BF16)

    x = x_ref[0]
    h = _rms(x, gpre_ref[...]).astype(BF16)
    uv = jax.nn.gelu(jnp.dot(h, wug_ref[:, :2 * SGU_WIDTH], preferred_element_type=F32))
    u = uv[:, :SGU_WIDTH]
    v = uv[:, SGU_WIDTH:]
    mu = jnp.mean(v, axis=-1, keepdims=True)
    vc = v - mu
    var = jnp.mean(vc * vc, axis=-1, keepdims=True)
    vn = vc * lax.rsqrt(var + EPS) * gsgu_ref[...] + bsgu_ref[...]

    t_idx = lax.broadcasted_iota(jnp.int32, (SGU_WINDOW, SGU_WINDOW), 0)
    s_idx = lax.broadcasted_iota(jnp.int32, (SGU_WINDOW, SGU_WINDOW), 1)
    wmask = (s_idx // CHUNK) <= (t_idx // CHUNK)
    lo = lax.broadcasted_iota(jnp.int32, (SGU_WINDOW, LANES), 1) < (SGU_WIDTH // SGU_GROUPS)
    for gp in range(SGU_GROUPS // 2):
        w_pair = jnp.concatenate(
            [jnp.where(wmask, ws_ref[2 * gp], 0.0), jnp.where(wmask, ws_ref[2 * gp + 1], 0.0)],
            axis=1).astype(BF16)
        cols = []
        for w in range(TM_MIX // SGU_WINDOW):
            vp = vn[w * SGU_WINDOW:(w + 1) * SGU_WINDOW, gp * LANES:(gp + 1) * LANES]
            cols.append(jnp.concatenate([jnp.where(lo, vp, 0.0), jnp.where(lo, 0.0, vp)],
                                        axis=0).astype(BF16))
        mixed = jnp.dot(w_pair, jnp.concatenate(cols, axis=1),
                        preferred_element_type=F32)
        for w in range(TM_MIX // SGU_WINDOW):
            mixed_ref[w * SGU_WINDOW:(w + 1) * SGU_WINDOW, gp * LANES:(gp + 1) * LANES] = (
                mixed[:, w * LANES:(w + 1) * LANES] + bsp_ref[:, gp * LANES:(gp + 1) * LANES])
    sgu = (u * mixed_ref[...]).astype(BF16)

    y_b = jnp.dot(sgu, wb_ref[...], preferred_element_type=F32)
    aT = aT_ref[0, :, 0].reshape(FOX_WIDTH, TM_MIX)
    y_a = lax.dot_general(aT, wa_ref[...], (((0,), (0,)), ((), ())),
                          preferred_element_type=F32)
    gates = jax.nn.sigmoid(jnp.dot(h, wug_ref[:, 2 * SGU_WIDTH:], preferred_element_type=F32))
    merged = (gates[:, :D_MODEL] * y_a + gates[:, D_MODEL:] * y_b).astype(BF16)
    o = jnp.dot(merged, wout_ref[...], preferred_element_type=F32)
    o_ref[0] = x + _rms(o, gpost_ref[...])


def _mix_call(x, aT, gpre, wug, gsgu, bsgu, ws, bsp, wa, wb, wout, layer, gpost):
    b, s, _ = x.shape
    const = lambda shape: pl.BlockSpec(shape, lambda bi, si: (0,) * len(shape))
    layer_w = lambda shape: pl.BlockSpec((1,) + shape, lambda bi, si: (layer, 0, 0),
                                         pipeline_mode=pl.Buffered(1))
    return pl.pallas_call(
        _mix_kernel,
        grid=(b, s // TM_MIX),
        in_specs=[
            pl.BlockSpec((1, TM_MIX, D_MODEL), lambda bi, si: (bi, si, 0)),
            pl.BlockSpec((1, HEADS, 1, HEAD_DIM, TM_MIX),
                         lambda bi, si: (bi, 0, si // (TQ // TM_MIX), 0, si % (TQ // TM_MIX))),
            const((1, D_MODEL)),
            const((D_MODEL, 2 * SGU_WIDTH + 2 * D_MODEL)),
            const((1, SGU_WIDTH)),
            const((1, SGU_WIDTH)),
            const((SGU_GROUPS, SGU_WINDOW, SGU_WINDOW)),
            const((SGU_WINDOW, SGU_WIDTH)),
            layer_w((FOX_WIDTH, D_MODEL)),
            layer_w((SGU_WIDTH, D_MODEL)),
            layer_w((D_MODEL, D_MODEL)),
            const((1, D_MODEL)),
        ],
        out_specs=pl.BlockSpec((1, TM_MIX, D_MODEL), lambda bi, si: (bi, si, 0)),
        out_shape=jax.ShapeDtypeStruct(x.shape, F32),
        scratch_shapes=[pltpu.VMEM((TM_MIX, SGU_WIDTH), F32),
                        pltpu.VMEM((FOX_WIDTH, D_MODEL), BF16),
                        pltpu.VMEM((SGU_WIDTH, D_MODEL), BF16),
                        pltpu.VMEM((D_MODEL, D_MODEL), BF16)],
        compiler_params=pltpu.CompilerParams(
            dimension_semantics=("arbitrary", "arbitrary"), vmem_limit_bytes=VMEM_LIMIT),
        name="fox_mix",
    )(x, aT, gpre, wug, gsgu, bsgu, ws, bsp, wa, wb, wout, gpost)


def _ffn_kernel(x_ref, gpre_ref, win_ref, wdown_ref, gpost_ref, o_ref):
    x = x_ref[...]
    h = _rms(x, gpre_ref[...]).astype(BF16)
    ff = jnp.zeros((TM_FFN, D_MODEL), F32)
    for lo, hi in zip(FF_EDGES[:-1], FF_EDGES[1:]):
        g = jnp.dot(h, win_ref[:, lo:hi], preferred_element_type=F32)
        u = jnp.dot(h, win_ref[:, D_FF + lo:D_FF + hi], preferred_element_type=F32)
        a = (jax.nn.silu(g) * u).astype(BF16)
        ff = ff + jnp.dot(a, wdown_ref[lo:hi, :], preferred_element_type=F32)
    o_ref[...] = x + _rms(ff, gpost_ref[...])


def _ffn_call(x, gpre, win, wdown, gpost):
    n, _ = x.shape
    const = lambda shape: pl.BlockSpec(shape, lambda i: (0,) * len(shape))
    return pl.pallas_call(
        _ffn_kernel,
        grid=(n // TM_FFN,),
        in_specs=[
            pl.BlockSpec((TM_FFN, D_MODEL), lambda i: (i, 0)),
            const((1, D_MODEL)),
            const((D_MODEL, 2 * D_FF)),
            const((D_FF, D_MODEL)),
            const((1, D_MODEL)),
        ],
        out_specs=pl.BlockSpec((TM_FFN, D_MODEL), lambda i: (i, 0)),
        out_shape=jax.ShapeDtypeStruct(x.shape, F32),
        compiler_params=pltpu.CompilerParams(
            dimension_semantics=("arbitrary",), vmem_limit_bytes=VMEM_LIMIT),
        name="fox_ffn",
    )(x, gpre, win, wdown, gpost)


def _qkv_stage(x, g_pre, wqkv, wfT, b_forget, g_q, g_k):
    bf = jnp.broadcast_to(b_forget[:, None], (HEADS, LANES))
    gq = jnp.broadcast_to(jnp.tile(g_q, HEADS)[:, None], (FOX_WIDTH, LANES))
    gk = jnp.broadcast_to(jnp.tile(g_k, HEADS)[:, None], (FOX_WIDTH, LANES))
    return _qkv_call(x, g_pre[None, :], wqkv, wfT, bf, gq, gk)


def kernel(x, g_pre_mix, w_in, b_forget, g_q, g_k, g_sgu, b_sgu, w_spatial, b_spatial,
           w_branch_a, w_branch_b, w_out, g_post_mix, g_pre_ffn, w_ffn_in, w_ffn_down,
           g_post_ffn):
    bsz, s_len, _ = x.shape
    for layer in range(g_pre_mix.shape[0]):
        wT = jnp.transpose(w_in[layer])
        wqkv, wfT = _win_prep_call(wT)
        bsp = jnp.repeat(jnp.transpose(b_spatial[layer]), SGU_WIDTH // SGU_GROUPS, axis=1)

        qT, k, vT = _qkv_stage(x, g_pre_mix[layer], wqkv, wfT, b_forget[layer], g_q[layer],
                               g_k[layer])
        aT, wfin, wfdn, wug = _attn_call(qT, k, vT, w_ffn_in, w_ffn_down, layer, wT)
        x = _mix_call(x, aT, g_pre_mix[layer][None, :], wug, g_sgu[layer][None, :],
                      b_sgu[layer][None, :], w_spatial[layer], bsp,
                      w_branch_a, w_branch_b, w_out, layer, g_post_mix[layer][None, :])
        x = _ffn_call(x.reshape(bsz * s_len, D_MODEL), g_pre_ffn[layer][None, :], wfin, wfdn,
                      g_post_ffn[layer][None, :]).reshape(bsz, s_len, D_MODEL)
    return x
```

```python
import jax
import jax.numpy as jnp
from jax import lax
from jax.experimental import pallas as pl
from jax.experimental.pallas import tpu as pltpu

F32 = jnp.float32
BF16 = jnp.bfloat16

D_MODEL = 1024
CHUNK = 64
HEAD_DIM = 64
FOX_WIDTH = 512
HEADS = 8
SGU_WIDTH = 512
SGU_GROUPS = 8
SGU_WINDOW = 128
D_FF = 2816
EPS = 1e-6

Q_OFF = 0
K_OFF = 512
V_OFF = 1024
F_OFF = 1536
U_OFF = 1544
G_OFF = 2568

LANES = 128
HEAD_PAD = LANES
N_SPLIT = 3

TM_QKV = 512
QKV_SUB = 4
TQ = TM_QKV
TK = TM_QKV
HALF = TK // 2
N_Q = 4096 // TQ
HPS = 2
LAG2 = 2
LAG3 = 4
SLOT_ORDER = (3, 2, 1)
RING = 3
V_ROWS = HEAD_DIM + 16
LOG2E = 1.4426950408889634
TM_MIX = TQ
MIX_SUB = 2
TM_FFN = 512
FFN_SUB = 2
MXU_DIM = 256
FF_EDGES = (0, 6 * MXU_DIM, D_FF)
NEG = -1e30

VMEM_LIMIT = 56 * 1024 * 1024


def _rms(x, g):
    ms = jnp.mean(x * x, axis=-1, keepdims=True)
    return x * lax.rsqrt(ms + EPS) * g


def _split3(d):
    d1 = d.astype(BF16)
    r1 = d - d1.astype(F32)
    d2 = r1.astype(BF16)
    d3 = (r1 - d2.astype(F32)).astype(BF16)
    return d1, d2, d3


def _win_prep_kernel(wT_ref, wqkv_ref, wfT_ref):
    wqkv_ref[...] = wT_ref[Q_OFF:F_OFF, :].T.astype(BF16)
    wfT_ref[...] = wT_ref[F_OFF:F_OFF + 2 * HEADS, :].astype(BF16)


def _win_prep_call(wT):
    _, d = wT.shape
    chunk = d // 4
    return pl.pallas_call(
        _win_prep_kernel,
        grid=(d // chunk,),
        in_specs=[pl.BlockSpec((F_OFF + 2 * HEADS, chunk), lambda i: (0, i))],
        out_specs=[pl.BlockSpec((chunk, F_OFF), lambda i: (i, 0)),
                   pl.BlockSpec((2 * HEADS, chunk), lambda i: (0, i))],
        out_shape=[jax.ShapeDtypeStruct((d, F_OFF), BF16),
                   jax.ShapeDtypeStruct((2 * HEADS, d), BF16)],
        compiler_params=pltpu.CompilerParams(
            dimension_semantics=("arbitrary",), vmem_limit_bytes=VMEM_LIMIT),
        name="fox_win_prep",
    )(wT)


def _qkv_kernel(x_ref, gpre_ref, wqkv_ref, wf_ref, bf_ref, gq_ref, gk_ref,
                qT_ref, k_ref, vT_ref, carry_ref):
    @pl.when(pl.program_id(1) == 0)
    def _():
        carry_ref[...] = jnp.zeros_like(carry_ref)

    sub = lax.broadcasted_iota(jnp.int32, (8, TM_QKV), 0)
    pad = jnp.zeros((HEAD_PAD - HEAD_DIM - 8, TM_QKV), F32)
    lane = lax.broadcasted_iota(jnp.int32, (HEADS, TM_QKV), 1)
    ones_rows = (lax.broadcasted_iota(jnp.int32, (V_ROWS - HEAD_DIM, TM_QKV), 0) == 0).astype(BF16)

    def tile(c, carry):
        tok = slice(c * TM_QKV, (c + 1) * TM_QKV)
        x = x_ref[0, tok, :]
        h = _rms(x, gpre_ref[...]).astype(BF16)

        f = lax.dot_general(wf_ref[...], h, (((1,), (1,)), ((), ())),
                            preferred_element_type=F32)[:HEADS, :]
        f = f + jnp.concatenate([bf_ref[...]] * (TM_QKV // LANES), axis=1)
        dT = jnp.minimum(f, 0.0) - jnp.log1p(jnp.exp(-jnp.abs(f)))
        shift = 1
        while shift < TM_QKV:
            dT = dT + jnp.where(lane >= shift, pltpu.roll(dT, shift, 1), 0.0)
            shift *= 2
        dT = dT + jnp.concatenate([carry] * (TM_QKV // LANES), axis=1)
        carry = jnp.broadcast_to(dT[:, TM_QKV - 1:TM_QKV], carry_ref.shape)
        d1, d2, d3 = (p.astype(F32) for p in _split3(dT * LOG2E))

        def head_aug_t(cols, g_ref, g_scale, hd, is_q):
            rows = cols[hd * HEAD_DIM:(hd + 1) * HEAD_DIM, :]
            g = jnp.concatenate([g_ref[hd * HEAD_DIM:(hd + 1) * HEAD_DIM, :] * g_scale]
                                * (TM_QKV // LANES), axis=1)
            ms = jnp.mean(rows * rows, axis=0, keepdims=True)
            normed = rows * lax.rsqrt(ms + EPS) * g
            sgn = 1.0 if is_q else -1.0
            pieces = jnp.where(sub % N_SPLIT == 0, d1[hd:hd + 1],
                               jnp.where(sub % N_SPLIT == 1, d2[hd:hd + 1], d3[hd:hd + 1])) * sgn
            d_rows = (sub < N_SPLIT) if is_q else ((sub >= N_SPLIT) & (sub < 2 * N_SPLIT))
            aug = jnp.where(d_rows, pieces, jnp.where(sub < 2 * N_SPLIT, 1.0, 0.0))
            return jnp.concatenate([normed, aug, pad], axis=0)

        qT = jnp.dot(h, wqkv_ref[:, Q_OFF:K_OFF], preferred_element_type=F32).T
        for hd in range(HEADS):
            qT_ref[0, hd, c] = head_aug_t(qT, gq_ref, HEAD_DIM ** -0.5 * LOG2E, hd,
                                          True).astype(BF16)
        kT = jnp.dot(h, wqkv_ref[:, K_OFF:V_OFF], preferred_element_type=F32).T
        for hd in range(HEADS):
            k_ref[0, tok, hd * HEAD_PAD:(hd + 1) * HEAD_PAD] = (
                head_aug_t(kT, gk_ref, 1.0, hd, False).T.astype(BF16))

        vT = jnp.dot(h, wqkv_ref[:, V_OFF:F_OFF], preferred_element_type=F32).T
        for hd in range(HEADS):
            vT_ref[0, hd, c, :HEAD_DIM, :] = vT[hd * HEAD_DIM:(hd + 1) * HEAD_DIM, :].astype(BF16)
            vT_ref[0, hd, c, HEAD_DIM:, :] = ones_rows
        return carry

    carry = carry_ref[...]
    for c in range(QKV_SUB):
        carry = tile(c, carry)
    carry_ref[...] = carry


def _qkv_call(x, gpre, wqkv, wf, bf, gq, gk):
    b, s, _ = x.shape
    n_s = s // TM_QKV
    const = lambda shape: pl.BlockSpec(shape, lambda bi, si: (0,) * len(shape))
    return pl.pallas_call(
        _qkv_kernel,
        grid=(b, n_s // QKV_SUB),
        in_specs=[
            pl.BlockSpec((1, QKV_SUB * TM_QKV, D_MODEL), lambda bi, si: (bi, si, 0)),
            const((1, D_MODEL)),
            const((D_MODEL, 3 * FOX_WIDTH)),
            const((2 * HEADS, D_MODEL)),
            const((HEADS, LANES)),
            const((FOX_WIDTH, LANES)),
            const((FOX_WIDTH, LANES)),
        ],
        out_specs=[
            pl.BlockSpec((1, HEADS, QKV_SUB, HEAD_PAD, TM_QKV), lambda bi, si: (bi, 0, si, 0, 0)),
            pl.BlockSpec((1, QKV_SUB * TM_QKV, HEADS * HEAD_PAD), lambda bi, si: (bi, si, 0)),
            pl.BlockSpec((1, HEADS, QKV_SUB, V_ROWS, TM_QKV), lambda bi, si: (bi, 0, si, 0, 0)),
        ],
        out_shape=[
            jax.ShapeDtypeStruct((b, HEADS, n_s, HEAD_PAD, TM_QKV), BF16),
            jax.ShapeDtypeStruct((b, s, HEADS * HEAD_PAD), BF16),
            jax.ShapeDtypeStruct((b, HEADS, n_s, V_ROWS, TM_QKV), BF16),
        ],
        scratch_shapes=[pltpu.VMEM((HEADS, LANES), F32)],
        compiler_params=pltpu.CompilerParams(
            dimension_semantics=("arbitrary", "arbitrary"), vmem_limit_bytes=VMEM_LIMIT),
        name="fox_qkv",
    )(x, gpre, wqkv, wf, bf, gq, gk)


def _attn_kernel(qT_ref, k_ref, vT_ref, wfin32_ref, wfdn32_ref, wT_ref,
                 o_ref, wfin_ref, wfdn_ref, wug_ref, *scratch):
    wfin_ref[...] = wfin32_ref[0].astype(BF16)
    wfdn_ref[...] = wfdn32_ref[0].astype(BF16)
    wug_ref[...] = wT_ref[U_OFF:, :].T.astype(BF16)

    sb = scratch[:RING]
    pb = scratch[RING:2 * RING]
    m_ref, acc_ref, bias_ref = scratch[2 * RING:]

    @pl.when((pl.program_id(0) == 0) & (pl.program_id(1) == 0))
    def _():
        row = lax.broadcasted_iota(jnp.int32, (TK, TQ), 0)
        col = lax.broadcasted_iota(jnp.int32, (TK, TQ), 1)
        bias_ref[...] = jnp.where(row <= col, 0.0, NEG)


    def s1(t, r, diag):
        qi, j = t
        start = j * TK if isinstance(j, int) else pl.multiple_of(j * TK, TK)
        out = []
        for hh in range(HPS):
            lanes = slice(hh * HEAD_PAD, (hh + 1) * HEAD_PAD)
            if diag:
                top = jnp.dot(k_ref[0, pl.ds(start, HALF), lanes], qT_ref[0, hh, qi],
                              preferred_element_type=F32) + bias_ref[:HALF, :]
                bot = jnp.dot(k_ref[0, pl.ds(start + HALF, HALF), lanes],
                              qT_ref[0, hh, qi, :, HALF:],
                              preferred_element_type=F32) + bias_ref[HALF:, HALF:]
                sb[r][hh, :HALF, :] = top
                sb[r][hh, HALF:, HALF:] = bot
                top_max = jnp.max(top, axis=0, keepdims=True)
                right = jnp.maximum(top_max[:, HALF:], jnp.max(bot, axis=0, keepdims=True))
                m_before = jnp.full((1, TQ), NEG, F32)
                m_after = jnp.concatenate([top_max[:, :HALF], right], axis=1)
            else:
                s = jnp.dot(k_ref[0, pl.ds(start, TK), lanes], qT_ref[0, hh, qi],
                            preferred_element_type=F32)
                sb[r][hh] = s
                m_before = m_ref[hh, qi]
                m_after = jnp.maximum(m_before, jnp.max(s, axis=0, keepdims=True))
            m_ref[hh, qi] = m_after
            out.append((m_before, m_after))
        return tuple(out)

    def s2(r, ms, diag):
        out = []
        for hh, (m_before, m_after) in enumerate(ms):
            if diag:
                pb[r][hh, :HALF, :] = jnp.exp2(sb[r][hh, :HALF, :] - m_after).astype(BF16)
                pb[r][hh, HALF:, HALF:] = jnp.exp2(
                    sb[r][hh, HALF:, HALF:] - m_after[:, HALF:]).astype(BF16)
            else:
                pb[r][hh] = jnp.exp2(sb[r][hh] - m_after).astype(BF16)
            out.append(jnp.exp2(m_before - m_after))
        return tuple(out)

    def s3(t, r, resc, diag):
        qi, j = t
        for hh, a in enumerate(resc):
            if diag:
                left = jnp.dot(vT_ref[0, hh, j, :, :HALF], pb[r][hh, :HALF, :HALF],
                               preferred_element_type=F32)
                right = jnp.dot(vT_ref[0, hh, j], pb[r][hh, :, HALF:],
                                preferred_element_type=F32)
                acc_ref[hh, qi] = jnp.concatenate([left, right], axis=1)
            else:
                pv = jnp.dot(vT_ref[0, hh, j], pb[r][hh], preferred_element_type=F32)
                acc_ref[hh, qi] = a * acc_ref[hh, qi] + pv

    def pipeline(tiles, n_diag, successor):
        n_tiles = len(tiles)
        ms = {}
        resc = {}

        def run_slot(a1, a2, a3):
            out = {}
            for stage in SLOT_ORDER:
                if stage == 1 and a1 is not None:
                    out[1] = s1(*a1)
                if stage == 2 and a2 is not None:
                    out[2] = s2(*a2)
                if stage == 3 and a3 is not None:
                    s3(*a3)
            return out.get(1), out.get(2)

        def slot(u):
            n2 = u - LAG2
            n3 = u - LAG3
            a1 = (tiles[u], u % RING, u < n_diag) if u < n_tiles else None
            a2 = (n2 % RING, ms.pop(n2), n2 < n_diag) if 0 <= n2 < n_tiles else None
            a3 = (tiles[n3], n3 % RING, resc.pop(n3), n3 < n_diag) if 0 <= n3 < n_tiles else None
            ms_out, resc_out = run_slot(a1, a2, a3)
            if a1 is not None:
                ms[u] = ms_out
            if a2 is not None:
                resc[n2] = resc_out

        loop_lo = n_diag + LAG3
        n_iter = (n_tiles - loop_lo) // RING
        loop_hi = loop_lo + n_iter * RING
        for u in range(loop_lo):
            slot(u)

        def body(_, carry):
            hist, ms_q, resc_q = carry
            for i in range(RING):
                hist = hist[1:] + (successor(hist[-1]),)
                u = loop_lo + i
                ms_out, resc_out = run_slot((hist[-1], u % RING, False),
                                            ((u - LAG2) % RING, ms_q[0], False),
                                            (hist[0], (u - LAG3) % RING, resc_q[0], False))
                resc_q = resc_q[1:] + (resc_out,)
                ms_q = ms_q[1:] + (ms_out,)
            return hist, ms_q, resc_q

        if n_iter > 0:
            hist0 = tuple((jnp.int32(q), jnp.int32(j))
                          for q, j in tiles[loop_lo - LAG3 - 1:loop_lo])
            ms0 = tuple(ms.pop(n) for n in range(loop_lo - LAG2, loop_lo))
            resc0 = tuple(resc.pop(n) for n in range(loop_lo - LAG3, loop_lo - LAG2))
            _, ms1, resc1 = lax.fori_loop(0, n_iter, body, (hist0, ms0, resc0))
            ms.update(zip(range(loop_hi - LAG2, loop_hi), ms1))
            resc.update(zip(range(loop_hi - LAG3, loop_hi - LAG2), resc1))
        for u in range(loop_hi, n_tiles + LAG3):
            slot(u)

    def next_full(t):
        qi, j = t
        wrap = j + 1 == qi
        return jnp.where(wrap, qi + 1, qi), jnp.where(wrap, 0, j + 1)

    pipeline([(q, q) for q in range(N_Q)] + [(q, j) for q in range(1, N_Q) for j in range(q)],
             N_Q, next_full)

    for hh in range(HPS):
        for qi in range(N_Q):
            acc = acc_ref[hh, qi]
            o_ref[0, hh, qi] = (acc[:HEAD_DIM] / acc[HEAD_DIM:HEAD_DIM + 1]).astype(BF16)


def _attn_call(qT, k, vT, w_ffn_in, w_ffn_down, layer, wT):
    b = qT.shape[0]
    s = k.shape[1]
    n_hg = HEADS // HPS
    steps = b * n_hg
    _, d, ff2 = w_ffn_in.shape
    _, ff, _ = w_ffn_down.shape
    cols = wT.shape[0]
    slab = lambda bi, hi: (bi * n_hg + hi, 0)
    hold = steps * LANES // d
    ug_slab = lambda bi, hi: (bi * n_hg + hi) // hold
    return pl.pallas_call(
        _attn_kernel,
        grid=(b, n_hg),
        in_specs=[
            pl.BlockSpec((1, HPS, N_Q, HEAD_PAD, TQ), lambda bi, hi: (bi, hi, 0, 0, 0)),
            pl.BlockSpec((1, s, HPS * HEAD_PAD), lambda bi, hi: (bi, 0, hi)),
            pl.BlockSpec((1, HPS, N_Q, V_ROWS, TK), lambda bi, hi: (bi, hi, 0, 0, 0)),
            pl.BlockSpec((1, d // steps, ff2), lambda bi, hi: (layer,) + slab(bi, hi)),
            pl.BlockSpec((1, ff // steps, d), lambda bi, hi: (layer,) + slab(bi, hi)),
            pl.BlockSpec((cols, LANES), lambda bi, hi: (0, ug_slab(bi, hi))),
        ],
        out_specs=[
            pl.BlockSpec((1, HPS, N_Q, HEAD_DIM, TQ), lambda bi, hi: (bi, hi, 0, 0, 0)),
            pl.BlockSpec((d // steps, ff2), slab),
            pl.BlockSpec((ff // steps, d), slab),
            pl.BlockSpec((LANES, cols - U_OFF), lambda bi, hi: (ug_slab(bi, hi), 0)),
        ],
        out_shape=[
            jax.ShapeDtypeStruct((b, HEADS, N_Q, HEAD_DIM, TQ), BF16),
            jax.ShapeDtypeStruct((d, ff2), BF16),
            jax.ShapeDtypeStruct((ff, d), BF16),
            jax.ShapeDtypeStruct((d, cols - U_OFF), BF16),
        ],
        scratch_shapes=(
            [pltpu.VMEM((HPS, TK, TQ), F32)] * RING + [pltpu.VMEM((HPS, TK, TQ), BF16)] * RING
            + [pltpu.VMEM((HPS, N_Q, 1, TQ), F32), pltpu.VMEM((HPS, N_Q, V_ROWS, TQ), F32),
               pltpu.VMEM((TK, TQ), F32)]),
        compiler_params=pltpu.CompilerParams(
            dimension_semantics=("arbitrary", "arbitrary"),
            vmem_limit_bytes=VMEM_LIMIT),
        name="fox_attn",
    )(qT, k, vT, w_ffn_in, w_ffn_down, wT)


def _mix_kernel(x_ref, aT_ref, gpre_ref, wug_ref, gsgu_ref, bsgu_ref, ws_ref, bsp_ref,
                wa32_ref, wb32_ref, wout32_ref, gpost_ref, o_ref, mixed_ref,
                wa_ref, wb_ref, wout_ref):
    @pl.when((pl.program_id(0) == 0) & (pl.program_id(1) == 0))
    def _():
        wa_ref[...] = wa32_ref[0].astype(BF16)
        wb_ref[...] = wb32_ref[0].astype(BF16)
        wout_ref[...] = wout32_ref[0].astype(BF16)

    t_idx = lax.broadcasted_iota(jnp.int32, (SGU_WINDOW, SGU_WINDOW), 0)
    s_idx = lax.broadcasted_iota(jnp.int32, (SGU_WINDOW, SGU_WINDOW), 1)
    wmask = (s_idx // CHUNK) <= (t_idx // CHUNK)
    lo = lax.broadcasted_iota(jnp.int32, (SGU_WINDOW, LANES), 1) < (SGU_WIDTH // SGU_GROUPS)
    w_pairs = [jnp.concatenate([jnp.where(wmask, ws_ref[2 * gp], 0.0),
                                jnp.where(wmask, ws_ref[2 * gp + 1], 0.0)], axis=1).astype(BF16)
               for gp in range(SGU_GROUPS // 2)]

    for c in range(MIX_SUB):
        tok = slice(c * TM_MIX, (c + 1) * TM_MIX)
        x = x_ref[0, tok, :]
        h = _rms(x, gpre_ref[...]).astype(BF16)
        uv = jax.nn.gelu(jnp.dot(h, wug_ref[:, :2 * SGU_WIDTH], preferred_element_type=F32))
        u = uv[:, :SGU_WIDTH]
        v = uv[:, SGU_WIDTH:]
        mu = jnp.mean(v, axis=-1, keepdims=True)
        vc = v - mu
        var = jnp.mean(vc * vc, axis=-1, keepdims=True)
        vn = vc * lax.rsqrt(var + EPS) * gsgu_ref[...] + bsgu_ref[...]

        for gp, w_pair in enumerate(w_pairs):
            cols = []
            for w in range(TM_MIX // SGU_WINDOW):
                vp = vn[w * SGU_WINDOW:(w + 1) * SGU_WINDOW, gp * LANES:(gp + 1) * LANES]
                cols.append(jnp.concatenate([jnp.where(lo, vp, 0.0), jnp.where(lo, 0.0, vp)],
                                            axis=0).astype(BF16))
            mixed = jnp.dot(w_pair, jnp.concatenate(cols, axis=1),
                            preferred_element_type=F32)
            for w in range(TM_MIX // SGU_WINDOW):
                mixed_ref[c, w * SGU_WINDOW:(w + 1) * SGU_WINDOW, gp * LANES:(gp + 1) * LANES] = (
                    mixed[:, w * LANES:(w + 1) * LANES] + bsp_ref[:, gp * LANES:(gp + 1) * LANES])
        sgu = (u * mixed_ref[c]).astype(BF16)

        y_b = jnp.dot(sgu, wb_ref[...], preferred_element_type=F32)
        aT = aT_ref[0, :, c].reshape(FOX_WIDTH, TM_MIX)
        y_a = lax.dot_general(aT, wa_ref[...], (((0,), (0,)), ((), ())),
                              preferred_element_type=F32)
        gates = jax.nn.sigmoid(jnp.dot(h, wug_ref[:, 2 * SGU_WIDTH:],
                                       preferred_element_type=F32))
        merged = (gates[:, :D_MODEL] * y_a + gates[:, D_MODEL:] * y_b).astype(BF16)
        o = jnp.dot(merged, wout_ref[...], preferred_element_type=F32)
        o_ref[0, tok, :] = x + _rms(o, gpost_ref[...])


def _mix_call(x, aT, gpre, wug, gsgu, bsgu, ws, bsp, wa, wb, wout, layer, gpost):
    b, s, _ = x.shape
    const = lambda shape: pl.BlockSpec(shape, lambda bi, si: (0,) * len(shape))
    layer_w = lambda shape: pl.BlockSpec((1,) + shape, lambda bi, si: (layer, 0, 0),
                                         pipeline_mode=pl.Buffered(1))
    return pl.pallas_call(
        _mix_kernel,
        grid=(b, s // (MIX_SUB * TM_MIX)),
        in_specs=[
            pl.BlockSpec((1, MIX_SUB * TM_MIX, D_MODEL), lambda bi, si: (bi, si, 0)),
            pl.BlockSpec((1, HEADS, MIX_SUB, HEAD_DIM, TM_MIX), lambda bi, si: (bi, 0, si, 0, 0)),
            const((1, D_MODEL)),
            pl.BlockSpec((D_MODEL, 2 * SGU_WIDTH + 2 * D_MODEL), lambda bi, si: (0, 0),
                         pipeline_mode=pl.Buffered(1)),
            const((1, SGU_WIDTH)),
            const((1, SGU_WIDTH)),
            const((SGU_GROUPS, SGU_WINDOW, SGU_WINDOW)),
            const((SGU_WINDOW, SGU_WIDTH)),
            layer_w((FOX_WIDTH, D_MODEL)),
            layer_w((SGU_WIDTH, D_MODEL)),
            layer_w((D_MODEL, D_MODEL)),
            const((1, D_MODEL)),
        ],
        out_specs=pl.BlockSpec((1, MIX_SUB * TM_MIX, D_MODEL), lambda bi, si: (bi, si, 0)),
        out_shape=jax.ShapeDtypeStruct(x.shape, F32),
        scratch_shapes=[pltpu.VMEM((MIX_SUB, TM_MIX, SGU_WIDTH), F32),
                        pltpu.VMEM((FOX_WIDTH, D_MODEL), BF16),
                        pltpu.VMEM((SGU_WIDTH, D_MODEL), BF16),
                        pltpu.VMEM((D_MODEL, D_MODEL), BF16)],
        compiler_params=pltpu.CompilerParams(
            dimension_semantics=("arbitrary", "arbitrary"), vmem_limit_bytes=VMEM_LIMIT),
        name="fox_mix",
    )(x, aT, gpre, wug, gsgu, bsgu, ws, bsp, wa, wb, wout, gpost)


def _ffn_kernel(x_ref, gpre_ref, win_ref, wdown_ref, gpost_ref, o_ref):
    for c in range(FFN_SUB):
        tok = slice(c * TM_FFN, (c + 1) * TM_FFN)
        x = x_ref[tok, :]
        h = _rms(x, gpre_ref[...]).astype(BF16)
        ff = jnp.zeros((TM_FFN, D_MODEL), F32)
        for lo, hi in zip(FF_EDGES[:-1], FF_EDGES[1:]):
            g = jnp.dot(h, win_ref[:, lo:hi], preferred_element_type=F32)
            u = jnp.dot(h, win_ref[:, D_FF + lo:D_FF + hi], preferred_element_type=F32)
            a = (jax.nn.silu(g) * u).astype(BF16)
            ff = ff + jnp.dot(a, wdown_ref[lo:hi, :], preferred_element_type=F32)
        o_ref[tok, :] = x + _rms(ff, gpost_ref[...])


def _ffn_call(x, gpre, win, wdown, gpost):
    n, _ = x.shape
    const = lambda shape: pl.BlockSpec(shape, lambda i: (0,) * len(shape))
    weight = lambda shape: pl.BlockSpec(shape, lambda i: (0,) * len(shape),
                                        pipeline_mode=pl.Buffered(1))
    return pl.pallas_call(
        _ffn_kernel,
        grid=(n // (FFN_SUB * TM_FFN),),
        in_specs=[
            pl.BlockSpec((FFN_SUB * TM_FFN, D_MODEL), lambda i: (i, 0)),
            const((1, D_MODEL)),
            weight((D_MODEL, 2 * D_FF)),
            weight((D_FF, D_MODEL)),
            const((1, D_MODEL)),
        ],
        out_specs=pl.BlockSpec((FFN_SUB * TM_FFN, D_MODEL), lambda i: (i, 0)),
        out_shape=jax.ShapeDtypeStruct(x.shape, F32),
        compiler_params=pltpu.CompilerParams(
            dimension_semantics=("arbitrary",), vmem_limit_bytes=VMEM_LIMIT),
        name="fox_ffn",
    )(x, gpre, win, wdown, gpost)


def _qkv_stage(x, g_pre, wqkv, wfT, b_forget, g_q, g_k):
    bf = jnp.broadcast_to(b_forget[:, None], (HEADS, LANES))
    gq = jnp.broadcast_to(jnp.tile(g_q, HEADS)[:, None], (FOX_WIDTH, LANES))
    gk = jnp.broadcast_to(jnp.tile(g_k, HEADS)[:, None], (FOX_WIDTH, LANES))
    return _qkv_call(x, g_pre[None, :], wqkv, wfT, bf, gq, gk)


def kernel(x, g_pre_mix, w_in, b_forget, g_q, g_k, g_sgu, b_sgu, w_spatial, b_spatial,
           w_branch_a, w_branch_b, w_out, g_post_mix, g_pre_ffn, w_ffn_in, w_ffn_down,
           g_post_ffn):
    bsz, s_len, _ = x.shape
    for layer in range(g_pre_mix.shape[0]):
        wT = jnp.transpose(w_in[layer])
        wqkv, wfT = _win_prep_call(wT)
        bsp = jnp.repeat(jnp.transpose(b_spatial[layer]), SGU_WIDTH // SGU_GROUPS, axis=1)

        qT, k, vT = _qkv_stage(x, g_pre_mix[layer], wqkv, wfT, b_forget[layer], g_q[layer],
                               g_k[layer])
        aT, wfin, wfdn, wug = _attn_call(qT, k, vT, w_ffn_in, w_ffn_down, layer, wT)
        x = _mix_call(x, aT, g_pre_mix[layer][None, :], wug, g_sgu[layer][None, :],
                      b_sgu[layer][None, :], w_spatial[layer], bsp,
                      w_branch_a, w_branch_b, w_out, layer, g_post_mix[layer][None, :])
        x = _ffn_call(x.reshape(bsz * s_len, D_MODEL), g_pre_ffn[layer][None, :], wfin, wfdn,
                      g_post_ffn[layer][None, :]).reshape(bsz, s_len, D_MODEL)
    return x
```
